```python
import math
import jax, jax.numpy as jnp
from jax import lax
import numpy as np


D_MODEL = 4096
BATCH = 4
SEQ = 2048
DEPTH = 2
DEC_BATCH = 128
DEC_SEQ = 4
PAST_LEN = 16384
PAGE_SIZE = 128

N_MIXERS = 2
N_POOL_LAYERS = (DEPTH + N_MIXERS - 1) // N_MIXERS
N_RWKV_LAYERS = DEPTH // N_MIXERS
POOL_WINDOWS = (2, 4, 8, 16)
POOL_GROUPS = 4
POOL_GC = D_MODEL // POOL_GROUPS
POOL_BUF = max(POOL_WINDOWS) - 1
RWKV_HEAD = 64
RWKV_HEADS = D_MODEL // RWKV_HEAD
LORA_DECAY = 128
LORA_AAA = 128
LORA_GATE = 480
GN_EPS = 64e-5
PEER_HEADS = 8
N_KEYS = 128
N_EXPERTS = N_KEYS * N_KEYS
PEER_QDIM = 256
PEER_HALF = PEER_QDIM // 2
PEER_TOPK = 16
PEER_BLOCK = 64
LN_EPS = 1e-5
ALPHA = (2.0 * DEPTH) ** 0.25
BETA = (8.0 * DEPTH) ** -0.25

kernel_name = 'pool_rwkv7_peer_hybrid_step'


def _layer_norm(x, g, b):
    xf = x.astype(jnp.float32)
    mu = jnp.mean(xf, axis=-1, keepdims=True)
    var = jnp.mean(jnp.square(xf - mu), axis=-1, keepdims=True)
    return ((xf - mu) * lax.rsqrt(var + LN_EPS) * g + b).astype(x.dtype)


def _pool_mix(x, past, start, w_pool, scale):
    B, T, _ = x.shape
    ext = jnp.concatenate([past.astype(x.dtype), x], axis=1)
    cs = jnp.cumsum(ext.astype(jnp.float32), axis=1)
    cs = jnp.pad(cs, ((0, 0), (1, 0), (0, 0)))
    n_real = start + jnp.arange(T) + 1
    means = []
    for gi, w in enumerate(POOL_WINDOWS):
        ch = slice(gi * POOL_GC, (gi + 1) * POOL_GC)
        lo = POOL_BUF + 1 - w
        s = cs[:, POOL_BUF + 1:POOL_BUF + 1 + T, ch] - cs[:, lo:lo + T, ch]
        cnt = jnp.minimum(n_real, w).astype(jnp.float32)
        means.append(s / cnt[None, :, None])
    d = jnp.concatenate(means, axis=-1) - x.astype(jnp.float32)
    d = d.astype(x.dtype).reshape(B, T, POOL_GROUPS, POOL_GC)
    y = jnp.einsum('btgc,gce->btge', d, w_pool).reshape(B, T, D_MODEL) * scale
    return y, ext[:, -POOL_BUF:]


def _wkv_step(S, inp):
    r, dec, k, v, a_in, b_in = inp
    sa = jnp.einsum('bhij,bhj->bhi', S, a_in)
    S = S * dec[:, :, None, :] + sa[..., None] * b_in[:, :, None, :] + v[..., None] * k[:, :, None, :]
    return S, jnp.einsum('bhij,bhj->bhi', S, r)


def _rwkv7_mix(x, shift_past, wkv_past, mu, w_rkv, w0, w1, w2, a0, a1, a2, g1, g2,
               k_k, k_a, r_k, lnx_g, lnx_b, w_o):
    B, T, _ = x.shape
    f32 = jnp.float32
    x_prev = jnp.concatenate([shift_past[:, None, :].astype(x.dtype), x[:, :-1]], axis=1)
    xx = x_prev - x
    xr = x + xx * mu[0]
    xw = x + xx * mu[1]
    xk = x + xx * mu[2]
    xv = x + xx * mu[3]
    xa = x + xx * mu[4]
    xg = x + xx * mu[5]
    r = xr @ w_rkv[0]
    k = xk @ w_rkv[1]
    v = xv @ w_rkv[2]
    w_log = -jax.nn.softplus(-(w0 + jnp.tanh(xw @ w1) @ w2).astype(f32)) - 0.5
    decay = jnp.exp(-jnp.exp(w_log))
    a = jax.nn.sigmoid((a0 + (xa @ a1) @ a2).astype(f32))
    g = jax.nn.sigmoid(xg @ g1) @ g2

    def heads(t):
        return t.astype(f32).reshape(B, T, RWKV_HEADS, RWKV_HEAD)

    kk = heads(k * k_k)
    kk = kk * lax.rsqrt(jnp.maximum(jnp.sum(kk * kk, axis=-1, keepdims=True), 1e-24))
    k_mod = k.astype(f32) * (1.0 + (a - 1.0) * k_a)
    rh, kh, vh, ah, dh = heads(r), heads(k_mod), heads(v), heads(a), heads(decay)

    def tmaj(t):
        return jnp.moveaxis(t, 1, 0)

    S, y = lax.scan(_wkv_step, wkv_past.astype(f32),
                    (tmaj(rh), tmaj(dh), tmaj(kh), tmaj(vh), tmaj(-kk), tmaj(kk * ah)))
    y = jnp.moveaxis(y, 0, 1)
    m = jnp.mean(y, axis=-1, keepdims=True)
    var = jnp.mean(jnp.square(y - m), axis=-1, keepdims=True)
    y = (y - m) * lax.rsqrt(var + GN_EPS) * lnx_g.reshape(RWKV_HEADS, RWKV_HEAD) \
        + lnx_b.reshape(RWKV_HEADS, RWKV_HEAD)
    y = y + jnp.sum(rh * kh * r_k, axis=-1, keepdims=True) * vh
    out = (y.reshape(B, T, D_MODEL).astype(x.dtype) * g) @ w_o
    return out, x[:, -1], S


def _peer_ffn(x, w_q, keys, u_tab, v_tab):
    B, T, _ = x.shape
    n = B * T
    xt = x.reshape(n, D_MODEL)
    q = (xt @ w_q).astype(jnp.float32).reshape(n, PEER_HEADS, PEER_QDIM)
    kf = keys.astype(jnp.float32)
    s1 = jnp.einsum('nhd,kd->nhk', q[..., :PEER_HALF], kf[0])
    s2 = jnp.einsum('nhd,kd->nhk', q[..., PEER_HALF:], kf[1])
    t1, i1 = lax.top_k(s1, PEER_TOPK)
    t2, i2 = lax.top_k(s2, PEER_TOPK)
    cand = (t1[..., :, None] + t2[..., None, :]).reshape(n, PEER_HEADS, PEER_TOPK * PEER_TOPK)
    ts, sel = lax.top_k(cand, PEER_TOPK)
    e_idx = (jnp.take_along_axis(i1, sel // PEER_TOPK, axis=-1) * N_KEYS
             + jnp.take_along_axis(i2, sel % PEER_TOPK, axis=-1))
    gate = jax.nn.softmax(ts, axis=-1)
    nb = -(-n // PEER_BLOCK)
    pad = nb * PEER_BLOCK - n
    HK = PEER_HEADS * PEER_TOPK
    xb = jnp.pad(xt, ((0, pad), (0, 0))).reshape(nb, PEER_BLOCK, D_MODEL)
    eb = jnp.pad(e_idx.reshape(n, HK), ((0, pad), (0, 0))).reshape(nb, PEER_BLOCK, HK)
    gb = jnp.pad(gate.reshape(n, HK), ((0, pad), (0, 0))).reshape(nb, PEER_BLOCK, HK)

    def block(args):
        xs, es, gs = args
        h = jnp.einsum('nkd,nd->nk', u_tab[es], xs)
        act = (gs * jax.nn.gelu(h.astype(jnp.float32), approximate=False)).astype(xs.dtype)
        return jnp.einsum('nk,nkd->nd', act, v_tab[es])

    out = lax.map(block, (xb, eb, gb)).reshape(nb * PEER_BLOCK, D_MODEL)[:n]
    return out.reshape(B, T, D_MODEL)


def _trunk(x, pool_past, shift_past, wkv_past, start, params):
    (pool_w, pool_scale, rwkv_mu, rwkv_w_rkv, rwkv_w0, rwkv_w1, rwkv_w2, rwkv_a0, rwkv_a1,
     rwkv_a2, rwkv_g1, rwkv_g2, rwkv_k_k, rwkv_k_a, rwkv_r_k, rwkv_lnx_g, rwkv_lnx_b, rwkv_w_o,
     peer_w_q, peer_keys, peer_u, peer_v, ln_g, ln_b) = params
    new_pool, new_shift, new_wkv = [], [], []
    for layer in range(DEPTH):
        li = layer // N_MIXERS
        if layer % N_MIXERS == 0:
            h, buf = _pool_mix(x, pool_past[li], start, pool_w[li], pool_scale[li])
            new_pool.append(buf)
        else:
            h, sh, S = _rwkv7_mix(x, shift_past[li], wkv_past[li], rwkv_mu[li], rwkv_w_rkv[li],
                                  rwkv_w0[li], rwkv_w1[li], rwkv_w2[li], rwkv_a0[li], rwkv_a1[li],
                                  rwkv_a2[li], rwkv_g1[li], rwkv_g2[li], rwkv_k_k[li], rwkv_k_a[li],
                                  rwkv_r_k[li], rwkv_lnx_g[li], rwkv_lnx_b[li], rwkv_w_o[li])
            new_shift.append(sh)
            new_wkv.append(S)
        x = _layer_norm(ALPHA * x + h, ln_g[layer, 0], ln_b[layer, 0])
        f = _peer_ffn(x, peer_w_q[layer], peer_keys[layer], peer_u[layer], peer_v[layer])
        x = _layer_norm(ALPHA * x + f, ln_g[layer, 1], ln_b[layer, 1])
    return x, jnp.stack(new_pool), jnp.stack(new_shift), jnp.stack(new_wkv)


def setup_inputs(seed: int = 0) -> dict:
    key = jax.random.key(seed)
    ks = list(jax.random.split(key, 32))
    f32 = jnp.float32

    def nrm(k, shape, scale):
        return jax.random.normal(k, shape, f32) * scale

    LP, LR, D = N_POOL_LAYERS, N_RWKV_LAYERS, D_MODEL
    rkv_scale = jnp.array([1.0, 1.0, BETA], f32)[None, :, None, None] * D ** -0.5
    return {
        'x_prompt': nrm(ks[0], (BATCH, SEQ, D), 1.0),
        'x_sample': nrm(ks[1], (DEC_BATCH, DEC_SEQ, D), 1.0),
        'state_pool': nrm(ks[2], (LP, DEC_BATCH, POOL_BUF, D), 1.0),
        'state_rwkv_shift': nrm(ks[3], (LR, DEC_BATCH, D), 1.0),
        'state_rwkv_wkv': nrm(ks[4], (LR, DEC_BATCH, RWKV_HEADS, RWKV_HEAD, RWKV_HEAD), 0.3),
        'pool_w': nrm(ks[5], (LP, POOL_GROUPS, POOL_GC, POOL_GC), POOL_GC ** -0.5 * BETA),
        'pool_scale': 1.0 + nrm(ks[6], (LP, D), 0.1),
        'rwkv_mu': jax.random.uniform(ks[7], (LR, 6, D), f32),
        'rwkv_w_rkv': jax.random.normal(ks[8], (LR, 3, D, D), f32) * rkv_scale,
        'rwkv_w0': jax.random.uniform(ks[9], (LR, D), f32, -4.0, 1.0),
        'rwkv_w1': nrm(ks[10], (LR, D, LORA_DECAY), D ** -0.5),
        'rwkv_w2': nrm(ks[11], (LR, LORA_DECAY, D), 0.1 * LORA_DECAY ** -0.5),
        'rwkv_a0': nrm(ks[12], (LR, D), 0.5),
        'rwkv_a1': nrm(ks[13], (LR, D, LORA_AAA), D ** -0.5),
        'rwkv_a2': nrm(ks[14], (LR, LORA_AAA, D), 0.1 * LORA_AAA ** -0.5),
        'rwkv_g1': nrm(ks[15], (LR, D, LORA_GATE), D ** -0.5),
        'rwkv_g2': nrm(ks[16], (LR, LORA_GATE, D), LORA_GATE ** -0.5),
        'rwkv_k_k': 0.85 + nrm(ks[17], (LR, D), 0.05),
        'rwkv_k_a': 1.0 + nrm(ks[18], (LR, D), 0.05),
        'rwkv_r_k': nrm(ks[19], (LR, RWKV_HEADS, RWKV_HEAD), 0.1),
        'rwkv_lnx_g': 1.0 + nrm(ks[20], (LR, D), 0.05),
        'rwkv_lnx_b': nrm(ks[21], (LR, D), 0.01),
        'rwkv_w_o': nrm(ks[22], (LR, D, D), D ** -0.5 * BETA),
        'peer_w_q': nrm(ks[23], (DEPTH, D, PEER_HEADS * PEER_QDIM), D ** -0.5),
        'peer_keys': nrm(ks[24], (DEPTH, 2, N_KEYS, PEER_HALF), PEER_HALF ** -0.5),
        'peer_u': nrm(ks[25], (DEPTH, N_EXPERTS, D), D ** -0.5),
        'peer_v': nrm(ks[26], (DEPTH, N_EXPERTS, D), 0.5 * BETA),
        'ln_g': 1.0 + nrm(ks[27], (DEPTH, 2, D), 0.05),
        'ln_b': nrm(ks[28], (DEPTH, 2, D), 0.01),
    }


def reference(x_prompt, x_sample, state_pool, state_rwkv_shift, state_rwkv_wkv,
              pool_w, pool_scale, rwkv_mu, rwkv_w_rkv, rwkv_w0, rwkv_w1, rwkv_w2,
              rwkv_a0, rwkv_a1, rwkv_a2, rwkv_g1, rwkv_g2, rwkv_k_k, rwkv_k_a, rwkv_r_k,
              rwkv_lnx_g, rwkv_lnx_b, rwkv_w_o, peer_w_q, peer_keys, peer_u, peer_v,
              ln_g, ln_b):
    params = (pool_w, pool_scale, rwkv_mu, rwkv_w_rkv, rwkv_w0, rwkv_w1, rwkv_w2, rwkv_a0,
              rwkv_a1, rwkv_a2, rwkv_g1, rwkv_g2, rwkv_k_k, rwkv_k_a, rwkv_r_k, rwkv_lnx_g,
              rwkv_lnx_b, rwkv_w_o, peer_w_q, peer_keys, peer_u, peer_v, ln_g, ln_b)
    B = x_prompt.shape[0]
    pool0 = jnp.zeros((N_POOL_LAYERS, B, POOL_BUF, D_MODEL), x_prompt.dtype)
    shift0 = jnp.zeros((N_RWKV_LAYERS, B, D_MODEL), x_prompt.dtype)
    wkv0 = jnp.zeros((N_RWKV_LAYERS, B, RWKV_HEADS, RWKV_HEAD, RWKV_HEAD), jnp.float32)
    y_prompt, pool_p, shift_p, wkv_p = _trunk(x_prompt, pool0, shift0, wkv0, 0, params)
    y_sample, pool_s, shift_s, wkv_s = _trunk(x_sample, state_pool, state_rwkv_shift,
                                              state_rwkv_wkv, PAST_LEN, params)
    return (y_prompt, y_sample, pool_p, pool_s, shift_p, shift_s, wkv_p, wkv_s)
```

```python
import functools
import math

import jax
import jax.numpy as jnp
from jax import lax
from jax.experimental import pallas as pl
from jax.experimental.pallas import tpu as pltpu

POOL_WINDOWS = (2, 4, 8, 16)
POOL_BUF = max(POOL_WINDOWS) - 1
POOL_HALO = POOL_BUF + 1
RWKV_HEAD = 64
GN_EPS = 64e-5
PEER_HEADS = 8
PEER_TOPK = 16
LN_EPS = 1e-5
DEPTH = 2
ALPHA = (2.0 * DEPTH) ** 0.25
PAST_LEN = 16384

V7X_LANES = 128
V7X_VMEM_LIMIT = 60 * 1024 * 1024
NEG_BIG = -3.0e38

_f32 = jnp.float32
_bf16 = jnp.bfloat16


def _tile(dim, pref, align=8):
    if dim <= pref:
        return dim
    for t in range(pref - pref % align, 0, -align):
        if dim % t == 0:
            return t
    raise ValueError((dim, pref, align))


def _params(sem):
    return pltpu.CompilerParams(dimension_semantics=sem, vmem_limit_bytes=V7X_VMEM_LIMIT)


def _layer_norm(z, g, b):
    mu = jnp.mean(z, axis=-1, keepdims=True)
    zc = z - mu
    var = jnp.mean(zc * zc, axis=-1, keepdims=True)
    return zc * lax.rsqrt(var + LN_EPS) * g + b


def _dot(a, b, precision=None):
    return jnp.dot(a, b, preferred_element_type=_f32, precision=precision)


def _dot_nt(a, b, precision=None):
    return lax.dot_general(a, b, (((1,), (1,)), ((), ())), preferred_element_type=_f32,
                           precision=precision)


def _dot_tn(a, b, precision=None):
    return lax.dot_general(a, b, (((0,), (0,)), ((), ())), preferred_element_type=_f32,
                           precision=precision)


def _pool_finish(x, window_sum, n_real, w_ref, scale_ref, g_ref, b_ref, z_ref):
    gc = x.shape[-1] // len(POOL_WINDOWS)
    for gi, w in enumerate(POOL_WINDOWS):
        cs = slice(gi * gc, (gi + 1) * gc)
        if isinstance(n_real, int):
            inv_cnt = 1.0 / float(min(n_real, w))
        else:
            inv_cnt = 1.0 / jnp.minimum(n_real, w).astype(_f32)
        d = window_sum(gi, w) * inv_cnt - x[:, cs]
        y = _dot(d.astype(_bf16), w_ref[gi]) * scale_ref[:, cs]
        z_ref[:, cs] = ALPHA * x[:, cs] + y
    return _layer_norm(z_ref[...], g_ref[...], b_ref[...])


def _pool_prompt_kernel(x_ref, halo_ref, w_ref, scale_ref, g_ref, b_ref, o_ref, ext_ref, *, start):
    tt = x_ref.shape[0]
    gc = x_ref.shape[1] // len(POOL_WINDOWS)
    ext_ref[0:POOL_HALO, :] = halo_ref[...]
    ext_ref[POOL_HALO:POOL_HALO + tt, :] = x_ref[...]
    t0 = pl.program_id(1) * tt
    n_real = lax.broadcasted_iota(jnp.int32, (tt, 1), 0) + (t0 + start + 1)

    def window_sum(gi, w):
        cs = slice(gi * gc, (gi + 1) * gc)
        s = ext_ref[POOL_HALO:POOL_HALO + tt, cs]
        for k in range(1, w):
            s = s + ext_ref[POOL_HALO - k:POOL_HALO - k + tt, cs]
        return s

    o_ref[...] = _pool_finish(x_ref[...], window_sum, n_real, w_ref, scale_ref, g_ref, b_ref, o_ref)


def _pool_prompt(x, past, start, w_bf, scale, g, b):
    B, T, D = x.shape
    tt = _tile(T, 256)
    nt = T // tt
    gc = D // len(POOL_WINDOWS)
    xe = jnp.concatenate([jnp.zeros((B, 1, D), x.dtype), past, x], axis=1)[:, :T]
    halo = xe.reshape(B, nt, tt, D)[:, :, :POOL_HALO]
    vec = pl.BlockSpec((1, D), lambda bi, ti: (0, 0))
    return pl.pallas_call(
        functools.partial(_pool_prompt_kernel, start=start),
        grid=(B, nt),
        in_specs=[pl.BlockSpec((None, tt, D), lambda bi, ti: (bi, ti, 0)),
                  pl.BlockSpec((None, None, POOL_HALO, D), lambda bi, ti: (bi, ti, 0, 0)),
                  pl.BlockSpec((len(POOL_WINDOWS), gc, gc), lambda bi, ti: (0, 0, 0)),
                  vec, vec, vec],
        out_specs=pl.BlockSpec((None, tt, D), lambda bi, ti: (bi, ti, 0)),
        out_shape=jax.ShapeDtypeStruct((B, T, D), _f32),
        scratch_shapes=[pltpu.VMEM((POOL_HALO + tt, D), _f32)],
        compiler_params=_params(("parallel", "arbitrary")),
        name="pool_prompt",
    )(x, halo, w_bf, scale, g, b)


def _pool_sample_kernel(ext_ref, w_ref, scale_ref, g_ref, b_ref, o_ref, z_ref, *, start):
    ts = o_ref.shape[0]
    gc = o_ref.shape[2] // len(POOL_WINDOWS)
    for t in range(ts):
        def window_sum(gi, w, t=t):
            cs = slice(gi * gc, (gi + 1) * gc)
            s = ext_ref[POOL_BUF + t, :, cs]
            for k in range(1, w):
                s = s + ext_ref[POOL_BUF + t - k, :, cs]
            return s

        o_ref[t] = _pool_finish(ext_ref[POOL_BUF + t], window_sum, start + t + 1, w_ref, scale_ref,
                                g_ref, b_ref, z_ref)


def _pool_sample(ext_tm, start, w_bf, scale, g, b):
    te, Bs, D = ext_tm.shape
    ts = te - POOL_BUF
    bb = _tile(Bs, 16)
    gc = D // len(POOL_WINDOWS)
    vec = pl.BlockSpec((1, D), lambda bi: (0, 0))
    return pl.pallas_call(
        functools.partial(_pool_sample_kernel, start=start),
        grid=(Bs // bb,),
        in_specs=[pl.BlockSpec((te, bb, D), lambda bi: (0, bi, 0)),
                  pl.BlockSpec((len(POOL_WINDOWS), gc, gc), lambda bi: (0, 0, 0)),
                  vec, vec, vec],
        out_specs=pl.BlockSpec((ts, bb, D), lambda bi: (0, bi, 0)),
        out_shape=jax.ShapeDtypeStruct((ts, Bs, D), _f32),
        scratch_shapes=[pltpu.VMEM((bb, D), _f32)],
        compiler_params=_params(("parallel",)),
        name="pool_sample",
    )(ext_tm, w_bf, scale, g, b)


def _top_rows(s, n):
    rows = []
    for r in range(n):
        m = jnp.max(s, axis=0, keepdims=True)
        rows.append(m)
        if r + 1 < n:
            s = jnp.where(s == m, NEG_BIG, s)
    return rows


def _max_tree(xs):
    xs = list(xs)
    while len(xs) > 1:
        nxt = [jnp.maximum(xs[i], xs[i + 1]) for i in range(0, len(xs) - 1, 2)]
        if len(xs) % 2:
            nxt.append(xs[-1])
        xs = nxt
    return xs[0]


def _peer_stats_kernel(x_ref, wq_ref, keys_ref, s2_ref, thr_ref, c_ref):
    half = keys_ref.shape[2]
    q = _dot(x_ref[...].astype(_bf16), wq_ref[...])
    s1 = _dot_nt(keys_ref[0], q[:, :half], precision=lax.Precision.HIGHEST)
    s2 = _dot_nt(keys_ref[1], q[:, half:], precision=lax.Precision.HIGHEST)
    n = PEER_TOPK + 1
    t1 = _top_rows(s1, n)
    t2 = _top_rows(s2, n)
    cands = [t1[a] + t2[b] for a in range(n) for b in range(n) if (a + 1) * (b + 1) <= n]
    c0 = t1[0] + t2[0]
    z = jnp.zeros_like(c0)
    vals = []
    for r in range(n):
        m = _max_tree(cands)
        vals.append(m)
        if r < PEER_TOPK:
            z = z + jnp.exp(m - c0)
        if r + 1 < n:
            cands = [jnp.where(c == m, NEG_BIG, c) for c in cands]
    tau = 0.5 * (vals[PEER_TOPK - 1] + vals[PEER_TOPK])
    s2_ref[...] = s2
    thr_ref[...] = tau - s1
    c_ref[...] = s1 - (c0 + jnp.log(z))


def _peer_stats(x, wq_bf, keys):
    N, D = x.shape
    n_keys, half = keys.shape[1], keys.shape[2]
    tn = _tile(N, 512)
    out = jax.ShapeDtypeStruct((PEER_HEADS, n_keys, N), _f32)
    ospec = pl.BlockSpec((None, n_keys, tn), lambda i, h: (h, 0, i))
    return pl.pallas_call(
        _peer_stats_kernel,
        grid=(N // tn, PEER_HEADS),
        in_specs=[pl.BlockSpec((tn, D), lambda i, h: (i, 0)),
                  pl.BlockSpec((D, 2 * half), lambda i, h: (0, h)),
                  pl.BlockSpec((2, n_keys, half), lambda i, h: (0, 0, 0))],
        out_specs=[ospec, ospec, ospec],
        out_shape=[out, out, out],
        compiler_params=_params(("parallel", "arbitrary")),
        name="peer_stats",
    )(x, wq_bf, keys)


def _gelu(h):
    return 0.5 * h * (1.0 + lax.erf(h * (1.0 / math.sqrt(2.0))))


def _peer_dense_kernel(x_ref, u_ref, v_ref, s2_ref, thr_ref, c_ref, g_ref, b_ref, o_ref,
                       xb_ref, at_ref):
    j = pl.program_id(1)
    n_keys = s2_ref.shape[1]
    te = u_ref.shape[0]
    sub = te // n_keys

    @pl.when(j == 0)
    def _():
        xb_ref[...] = x_ref[...].astype(_bf16)
        o_ref[...] = jnp.zeros_like(o_ref)

    ht = _dot_nt(u_ref[...], xb_ref[...])
    for ii in range(sub):
        i1 = j * sub + ii
        gate = None
        for h in range(PEER_HEADS):
            s2 = s2_ref[h]
            thr = thr_ref[h, pl.ds(i1, 1), :]
            c = c_ref[h, pl.ds(i1, 1), :]
            term = jnp.where(s2 > thr, jnp.exp(s2 + c), 0.0)
            gate = term if gate is None else gate + term
        rows = slice(ii * n_keys, (ii + 1) * n_keys)
        at_ref[rows, :] = (_gelu(ht[rows, :]) * gate).astype(_bf16)
    o_ref[...] += _dot_tn(at_ref[...], v_ref[...])

    @pl.when(j == pl.num_programs(1) - 1)
    def _():
        o_ref[...] = _layer_norm(ALPHA * x_ref[...] + o_ref[...], g_ref[...], b_ref[...])


def _peer_dense(x, u_bf, v_bf, s2t, thrt, ct, g, b):
    N, D = x.shape
    E = u_bf.shape[0]
    n_keys = s2t.shape[1]
    tn = _tile(N, 512)
    te = _tile(E, 2 * n_keys)
    vec = pl.BlockSpec((1, D), lambda i, j: (0, 0))
    once = pl.Buffered(1)
    stat = pl.BlockSpec((PEER_HEADS, n_keys, tn), lambda i, j: (0, 0, i), pipeline_mode=once)
    return pl.pallas_call(
        _peer_dense_kernel,
        grid=(N // tn, E // te),
        in_specs=[pl.BlockSpec((tn, D), lambda i, j: (i, 0), pipeline_mode=once),
                  pl.BlockSpec((te, D), lambda i, j: (j, 0)),
                  pl.BlockSpec((te, D), lambda i, j: (j, 0)),
                  stat, stat, stat, vec, vec],
        out_specs=pl.BlockSpec((tn, D), lambda i, j: (i, 0)),
        out_shape=jax.ShapeDtypeStruct((N, D), _f32),
        scratch_shapes=[pltpu.VMEM((tn, D), _bf16), pltpu.VMEM((te, tn), _bf16)],
        compiler_params=_params(("parallel", "arbitrary")),
        name="peer_dense",
    )(x, u_bf, v_bf, s2t, thrt, ct, g, b)


def _peer_layer(x, wq_bf, keys, u_bf, v_bf, g, b):
    s2t, thrt, ct = _peer_stats(x, wq_bf, keys)
    return _peer_dense(x, u_bf, v_bf, s2t, thrt, ct, g, b)


def _rwkv_mix_kernel(x_ref, xp_ref, mu_ref, w1_ref, a1_ref, g1_ref,
                     xr_ref, xk_ref, xv_ref, hw_ref, ha_ref, hg_ref):
    x = x_ref[...]
    xx = xp_ref[...] - x

    def mix(i):
        return (x + xx * mu_ref[i:i + 1, :]).astype(_bf16)

    xr_ref[...] = mix(0)
    xk_ref[...] = mix(2)
    xv_ref[...] = mix(3)
    hw_ref[...] = jnp.tanh(_dot(mix(1), w1_ref[...])).astype(_bf16)
    ha_ref[...] = _dot(mix(4), a1_ref[...]).astype(_bf16)
    hg_ref[...] = (1.0 / (1.0 + jnp.exp(-_dot(mix(5), g1_ref[...])))).astype(_bf16)


def _rwkv_mix(x, xp, mu, w1_bf, a1_bf, g1_bf):
    N, D = x.shape
    tm = _tile(N, 256)
    row = pl.BlockSpec((tm, D), lambda i: (i, 0))

    def full(a):
        return pl.BlockSpec(a.shape, lambda i: (0, 0))

    def out(cols, dt):
        return pl.BlockSpec((tm, cols), lambda i: (i, 0)), jax.ShapeDtypeStruct((N, cols), dt)

    outs = [out(D, _bf16), out(D, _bf16), out(D, _bf16), out(w1_bf.shape[1], _bf16),
            out(a1_bf.shape[1], _bf16), out(g1_bf.shape[1], _bf16)]
    return pl.pallas_call(
        _rwkv_mix_kernel,
        grid=(N // tm,),
        in_specs=[row, row, full(mu), full(w1_bf), full(a1_bf), full(g1_bf)],
        out_specs=[o[0] for o in outs],
        out_shape=[o[1] for o in outs],
        compiler_params=_params(("parallel",)),
        name="rwkv_mix",
    )(x, xp, mu, w1_bf, a1_bf, g1_bf)


def _rwkv_proj_kernel(xr_ref, xk_ref, xv_ref, wr_ref, wk_ref, wv_ref, hw_ref, ha_ref, hg_ref,
                      w2_ref, a2_ref, g2_ref, w0_ref, a0_ref,
                      r_ref, lw_ref, k_ref, v_ref, a_ref, g_ref):
    r_ref[...] = _dot(xr_ref[...], wr_ref[...])
    k_ref[...] = _dot(xk_ref[...], wk_ref[...])
    v_ref[...] = _dot(xv_ref[...], wv_ref[...])
    u = -(w0_ref[...] + _dot(hw_ref[...], w2_ref[...]))
    softplus = jnp.maximum(u, 0.0) + jnp.log(1.0 + jnp.exp(-jnp.abs(u)))
    lw_ref[...] = -jnp.exp(-softplus - 0.5)
    a_ref[...] = 1.0 / (1.0 + jnp.exp(-(a0_ref[...] + _dot(ha_ref[...], a2_ref[...]))))
    g_ref[...] = _dot(hg_ref[...], g2_ref[...])


def _rwkv_proj(xr, xk, xv, wrkv_bf, hw, ha, hg, w2_bf, a2_bf, g2_bf, w0, a0):
    N, D = xr.shape
    tm = _tile(N, 512)
    tc = _tile(D, 256)
    row = pl.BlockSpec((tm, D), lambda i, j: (i, 0))

    def wmat(m):
        return pl.BlockSpec((None, D, tc), lambda i, j, m=m: (m, 0, j))

    def rowfull(a):
        return pl.BlockSpec((tm, a.shape[1]), lambda i, j: (i, 0))

    def colblk(a):
        return pl.BlockSpec((a.shape[0], tc), lambda i, j: (0, j))

    ospec = pl.BlockSpec((tm, tc), lambda i, j: (i, j))
    oshape = jax.ShapeDtypeStruct((N, D), _f32)
    return pl.pallas_call(
        _rwkv_proj_kernel,
        grid=(N // tm, D // tc),
        in_specs=[row, row, row, wmat(0), wmat(1), wmat(2), rowfull(hw), rowfull(ha), rowfull(hg),
                  colblk(w2_bf), colblk(a2_bf), colblk(g2_bf), colblk(w0), colblk(a0)],
        out_specs=[ospec] * 6,
        out_shape=[oshape] * 6,
        compiler_params=_params(("parallel", "arbitrary")),
        name="rwkv_proj",
    )(xr, xk, xv, wrkv_bf, wrkv_bf, wrkv_bf, hw, ha, hg, w2_bf, a2_bf, g2_bf, w0, a0)


def _wkv_kernel(r_ref, lw_ref, k_ref, v_ref, a_ref, s0_ref, kk_ref, ka_ref, rk_ref, gg_ref, gb_ref,
                y_ref, s_ref):
    tc = r_ref.shape[0]
    hd = r_ref.shape[1]

    @pl.when(pl.program_id(1) == 0)
    def _():
        s_ref[...] = s0_ref[...]

    def step(t, carry):
        r = r_ref[t]
        k = k_ref[t]
        a = a_ref[t]
        v = v_ref[t]
        dec = jnp.exp(lw_ref[t])
        kk = k * kk_ref[...]
        kk = kk * lax.rsqrt(jnp.maximum(jnp.sum(kk * kk, axis=0, keepdims=True), 1e-24))
        kmod = k * (1.0 + (a - 1.0) * ka_ref[...])
        an = -kk
        bn = kk * a
        dr = dec * r
        br = jnp.sum(bn * r, axis=0, keepdims=True)
        kr = jnp.sum(kmod * r, axis=0, keepdims=True)
        sa = jnp.zeros_like(v)
        z = jnp.zeros_like(v)
        for j in range(hd):
            sj = s_ref[j]
            sa = sa + sj * an[j:j + 1, :]
            z = z + sj * dr[j:j + 1, :]
        y = z + sa * br + v * kr
        for j in range(hd):
            s_ref[j] = s_ref[j] * dec[j:j + 1, :] + sa * bn[j:j + 1, :] + v * kmod[j:j + 1, :]
        m = jnp.mean(y, axis=0, keepdims=True)
        yc = y - m
        var = jnp.mean(yc * yc, axis=0, keepdims=True)
        yn = yc * lax.rsqrt(var + GN_EPS) * gg_ref[...] + gb_ref[...]
        bonus = jnp.sum(r * kmod * rk_ref[...], axis=0, keepdims=True)
        y_ref[t] = yn + bonus * v
        return carry

    lax.fori_loop(0, tc, step, 0)


def _wkv(r, lw, k, v, a, s0, kk_t, ka_t, rk_t, gg_t, gb_t):
    T, hd, L = r.shape
    tl = _tile(L, V7X_LANES)
    tc = _tile(T, 32)
    op = pl.BlockSpec((tc, hd, tl), lambda g, c: (c, 0, g))
    st = pl.BlockSpec((hd, hd, tl), lambda g, c: (0, 0, g))
    par = pl.BlockSpec((hd, tl), lambda g, c: (0, g))
    return pl.pallas_call(
        _wkv_kernel,
        grid=(L // tl, T // tc),
        in_specs=[op, op, op, op, op, st, par, par, par, par, par],
        out_specs=[op, st],
        out_shape=[jax.ShapeDtypeStruct((T, hd, L), _f32), jax.ShapeDtypeStruct((hd, hd, L), _f32)],
        compiler_params=_params(("parallel", "arbitrary")),
        name="wkv_scan",
    )(r, lw, k, v, a, s0, kk_t, ka_t, rk_t, gg_t, gb_t)


def _rwkv_out_kernel(y_ref, g_ref, wo_ref, x_ref, lg_ref, lb_ref, o_ref):
    kt = pl.program_id(1)

    @pl.when(kt == 0)
    def _():
        o_ref[...] = jnp.zeros_like(o_ref)

    o_ref[...] += _dot((y_ref[...] * g_ref[...]).astype(_bf16), wo_ref[...])

    @pl.when(kt == pl.num_programs(1) - 1)
    def _():
        o_ref[...] = _layer_norm(ALPHA * x_ref[...] + o_ref[...], lg_ref[...], lb_ref[...])


def _rwkv_out(y, g, wo_bf, x, lg, lb):
    N, D = x.shape
    tm = _tile(N, 512)
    tk = _tile(D, 512)
    vec = pl.BlockSpec((1, D), lambda i, kt: (0, 0))
    return pl.pallas_call(
        _rwkv_out_kernel,
        grid=(N // tm, D // tk),
        in_specs=[pl.BlockSpec((tm, tk), lambda i, kt: (i, kt)),
                  pl.BlockSpec((tm, tk), lambda i, kt: (i, kt)),
                  pl.BlockSpec((tk, D), lambda i, kt: (kt, 0)),
                  pl.BlockSpec((tm, D), lambda i, kt: (i, 0)),
                  vec, vec],
        out_specs=pl.BlockSpec((tm, D), lambda i, kt: (i, 0)),
        out_shape=jax.ShapeDtypeStruct((N, D), _f32),
        compiler_params=_params(("parallel", "arbitrary")),
        name="rwkv_out",
    )(y, g, wo_bf, x, lg, lb)


def _to_lanes(a, lead, hd):
    T, B = lead
    H = a.shape[-1] // hd
    return a.reshape(T, B, H, hd).transpose(0, 3, 1, 2).reshape(T, hd, B * H)


def _from_lanes(a, lead, hd):
    T, B = lead
    H = a.shape[-1] // B
    return a.reshape(T, hd, B, H).transpose(0, 2, 3, 1).reshape(T, B, H * hd)


def _head_param(p, B, hd):
    H = p.shape[-1] // hd
    return jnp.tile(p.reshape(H, hd).T[:, None, :], (1, B, 1)).reshape(hd, B * H)


def kernel(x_prompt, x_sample, state_pool, state_rwkv_shift, state_rwkv_wkv, pool_w, pool_scale,
           rwkv_mu, rwkv_w_rkv, rwkv_w0, rwkv_w1, rwkv_w2, rwkv_a0, rwkv_a1, rwkv_a2, rwkv_g1,
           rwkv_g2, rwkv_k_k, rwkv_k_a, rwkv_r_k, rwkv_lnx_g, rwkv_lnx_b, rwkv_w_o, peer_w_q,
           peer_keys, peer_u, peer_v, ln_g, ln_b):
    B, T, D = x_prompt.shape
    Bs, Ts, _ = x_sample.shape
    hd = RWKV_HEAD
    H = D // hd
    n_p = B * T
    bf = lambda a: a.astype(_bf16)
    row = lambda a: a.reshape(1, D)

    pw = bf(pool_w[0])
    ps = row(pool_scale[0])
    g00, b00 = row(ln_g[0, 0]), row(ln_b[0, 0])
    past_p = jnp.zeros((B, POOL_BUF, D), x_prompt.dtype)
    x1p = _pool_prompt(x_prompt, past_p, 0, pw, ps, g00, b00)
    ext_s = jnp.concatenate([state_pool[0], x_sample], axis=1)
    x1s = _pool_sample(ext_s.transpose(1, 0, 2), PAST_LEN, pw, ps, g00, b00)
    x1 = jnp.concatenate([x1p.reshape(n_p, D), x1s.reshape(Ts * Bs, D)], axis=0)
    pool_p = x_prompt[:, T - POOL_BUF:][None]
    pool_s = ext_s[:, Ts:][None]

    x2 = _peer_layer(x1, bf(peer_w_q[0]), peer_keys[0], bf(peer_u[0]), bf(peer_v[0]),
                     row(ln_g[0, 1]), row(ln_b[0, 1]))

    x2p = x2[:n_p].reshape(B, T, D)
    x2s = x2[n_p:].reshape(Ts, Bs, D)
    xp_p = jnp.concatenate([jnp.zeros((B, 1, D), _f32), x2p[:, :-1]], axis=1)
    xp_s = jnp.concatenate([state_rwkv_shift[0][None], x2s[:-1]], axis=0)
    xprev = jnp.concatenate([xp_p.reshape(n_p, D), xp_s.reshape(Ts * Bs, D)], axis=0)
    shift_p = x2p[:, -1][None]
    shift_s = x2s[-1][None]

    g1 = rwkv_g1[0]
    g2 = rwkv_g2[0]
    gpad = (-g1.shape[1]) % V7X_LANES
    g1 = jnp.pad(g1, ((0, 0), (0, gpad)))
    g2 = jnp.pad(g2, ((0, gpad), (0, 0)))
    xr, xk, xv, hw, ha, hg = _rwkv_mix(x2, xprev, rwkv_mu[0], bf(rwkv_w1[0]), bf(rwkv_a1[0]), bf(g1))
    r, lw, k, v, a, g = _rwkv_proj(xr, xk, xv, bf(rwkv_w_rkv[0]), hw, ha, hg, bf(rwkv_w2[0]),
                                   bf(rwkv_a2[0]), bf(g2), row(rwkv_w0[0]), row(rwkv_a0[0]))

    def scan(time_major, s0_t):
        ops = [time_major(o) for o in (r, lw, k, v, a)]
        nt, nb = ops[0].shape[:2]
        ops = [_to_lanes(o, (nt, nb), hd) for o in ops]
        pars = [_head_param(p, nb, hd) for p in (rwkv_k_k[0], rwkv_k_a[0], rwkv_r_k[0].reshape(D),
                                                 rwkv_lnx_g[0], rwkv_lnx_b[0])]
        y, s_fin = _wkv(*ops, s0_t, *pars)
        y = _from_lanes(y, (nt, nb), hd)
        s_fin = s_fin.reshape(hd, hd, nb, H).transpose(2, 3, 1, 0)
        return y, s_fin

    y_p, wkv_p = scan(lambda o: o[:n_p].reshape(B, T, D).transpose(1, 0, 2),
                      jnp.zeros((hd, hd, B * H), _f32))
    s0_s = state_rwkv_wkv[0].transpose(3, 2, 0, 1).reshape(hd, hd, Bs * H)
    y_s, wkv_s = scan(lambda o: o[n_p:].reshape(Ts, Bs, D), s0_s)
    y = jnp.concatenate([y_p.transpose(1, 0, 2).reshape(n_p, D), y_s.reshape(Ts * Bs, D)], axis=0)
    x3 = _rwkv_out(y, g, bf(rwkv_w_o[0]), x2, row(ln_g[1, 0]), row(ln_b[1, 0]))

    x4 = _peer_layer(x3, bf(peer_w_q[1]), peer_keys[1], bf(peer_u[1]), bf(peer_v[1]),
                     row(ln_g[1, 1]), row(ln_b[1, 1]))
    y_prompt = x4[:n_p].reshape(B, T, D)
    y_sample = x4[n_p:].reshape(Ts, Bs, D).transpose(1, 0, 2)
    return (y_prompt, y_sample, pool_p, pool_s, shift_p, shift_s, wkv_p[None], wkv_s[None])
```

```python
import functools
import math

import jax
import jax.numpy as jnp
from jax import lax
from jax.experimental import pallas as pl
from jax.experimental.pallas import tpu as pltpu

POOL_WINDOWS = (2, 4, 8, 16)
POOL_BUF = max(POOL_WINDOWS) - 1
POOL_HALO = POOL_BUF + 1
RWKV_HEAD = 64
GN_EPS = 64e-5
PEER_HEADS = 8
PEER_TOPK = 16
PEER_GATE_ROWS = 16
LN_EPS = 1e-5
DEPTH = 2
ALPHA = (2.0 * DEPTH) ** 0.25
PAST_LEN = 16384

V7X_LANES = 128
V7X_SUBLANES = 8
LN_ROWS = 8
V7X_VMEM_LIMIT = 63 * 1024 * 1024
NEG_BIG = -3.0e38
LOG2E = 1.4426950408889634

_f32 = jnp.float32
_bf16 = jnp.bfloat16


def _tile(dim, pref, align=8):
    if dim <= pref:
        return dim
    for t in range(pref - pref % align, 0, -align):
        if dim % t == 0:
            return t
    raise ValueError((dim, pref, align))


def _params(sem):
    return pltpu.CompilerParams(dimension_semantics=sem, vmem_limit_bytes=V7X_VMEM_LIMIT)


def _layer_norm(z, g, b):
    mu = jnp.mean(z, axis=-1, keepdims=True)
    zc = z - mu
    var = jnp.mean(zc * zc, axis=-1, keepdims=True)
    return zc * lax.rsqrt(var + LN_EPS) * g + b


def _residual_layer_norm(o_ref, x_ref, g_ref, b_ref):
    n_rows = o_ref.shape[0]
    rows_per = LN_ROWS if n_rows % LN_ROWS == 0 else n_rows
    g = g_ref[...]
    b = b_ref[...]

    def body(i, carry):
        rows = pl.ds(pl.multiple_of(i * rows_per, rows_per), rows_per)
        z = o_ref[rows, :]
        if x_ref is not None:
            z = ALPHA * x_ref[rows, :] + z
        o_ref[rows, :] = _layer_norm(z, g, b)
        return carry

    lax.fori_loop(0, n_rows // rows_per, body, 0)


def _dot(a, b, precision=None):
    return jnp.dot(a, b, preferred_element_type=_f32, precision=precision)


def _dot_nt(a, b, precision=None):
    return lax.dot_general(a, b, (((1,), (1,)), ((), ())), preferred_element_type=_f32,
                           precision=precision)


def _dot_tn(a, b, precision=None):
    return lax.dot_general(a, b, (((0,), (0,)), ((), ())), preferred_element_type=_f32,
                           precision=precision)


def _pool_finish(x, window_sum, n_real, w_ref, scale_ref, g_ref, b_ref, z_ref):
    gc = x.shape[-1] // len(POOL_WINDOWS)
    for gi, w in enumerate(POOL_WINDOWS):
        cs = slice(gi * gc, (gi + 1) * gc)
        if isinstance(n_real, int):
            inv_cnt = 1.0 / float(min(n_real, w))
        else:
            inv_cnt = 1.0 / jnp.minimum(n_real, w).astype(_f32)
        d = window_sum(gi, w) * inv_cnt - x[:, cs]
        y = _dot(d.astype(_bf16), w_ref[gi]) * scale_ref[:, cs]
        z_ref[:, cs] = ALPHA * x[:, cs] + y
    _residual_layer_norm(z_ref, None, g_ref, b_ref)


def _pool_prompt_kernel(x_ref, halo_ref, w_ref, scale_ref, g_ref, b_ref, o_ref, ext_ref, *, start):
    tt = x_ref.shape[0]
    gc = x_ref.shape[1] // len(POOL_WINDOWS)
    ext_ref[0:POOL_HALO, :] = halo_ref[...]
    ext_ref[POOL_HALO:POOL_HALO + tt, :] = x_ref[...]
    t0 = pl.program_id(1) * tt
    n_real = lax.broadcasted_iota(jnp.int32, (tt, 1), 0) + (t0 + start + 1)

    def window_sum(gi, w):
        cs = slice(gi * gc, (gi + 1) * gc)
        s = ext_ref[POOL_HALO:POOL_HALO + tt, cs]
        for k in range(1, w):
            s = s + ext_ref[POOL_HALO - k:POOL_HALO - k + tt, cs]
        return s

    _pool_finish(x_ref[...], window_sum, n_real, w_ref, scale_ref, g_ref, b_ref, o_ref)


def _pool_prompt(x, past, start, w_bf, scale, g, b):
    B, T, D = x.shape
    tt = _tile(T, 256)
    nt = T // tt
    gc = D // len(POOL_WINDOWS)
    xe = jnp.concatenate([jnp.zeros((B, 1, D), x.dtype), past, x], axis=1)[:, :T]
    halo = xe.reshape(B, nt, tt, D)[:, :, :POOL_HALO]
    vec = pl.BlockSpec((1, D), lambda bi, ti: (0, 0))
    return pl.pallas_call(
        functools.partial(_pool_prompt_kernel, start=start),
        grid=(B, nt),
        in_specs=[pl.BlockSpec((None, tt, D), lambda bi, ti: (bi, ti, 0)),
                  pl.BlockSpec((None, None, POOL_HALO, D), lambda bi, ti: (bi, ti, 0, 0)),
                  pl.BlockSpec((len(POOL_WINDOWS), gc, gc), lambda bi, ti: (0, 0, 0)),
                  vec, vec, vec],
        out_specs=pl.BlockSpec((None, tt, D), lambda bi, ti: (bi, ti, 0)),
        out_shape=jax.ShapeDtypeStruct((B, T, D), _f32),
        scratch_shapes=[pltpu.VMEM((POOL_HALO + tt, D), _f32)],
        compiler_params=_params(("parallel", "arbitrary")),
        name="pool_prompt",
    )(x, halo, w_bf, scale, g, b)


def _pool_sample_kernel(ext_ref, w_ref, scale_ref, g_ref, b_ref, o_ref, z_ref, *, start):
    ts = o_ref.shape[0]
    gc = o_ref.shape[2] // len(POOL_WINDOWS)
    for t in range(ts):
        def window_sum(gi, w, t=t):
            cs = slice(gi * gc, (gi + 1) * gc)
            s = ext_ref[POOL_BUF + t, :, cs]
            for k in range(1, w):
                s = s + ext_ref[POOL_BUF + t - k, :, cs]
            return s

        _pool_finish(ext_ref[POOL_BUF + t], window_sum, start + t + 1, w_ref, scale_ref,
                     g_ref, b_ref, z_ref)
        o_ref[t] = z_ref[...]


def _pool_sample(ext_tm, start, w_bf, scale, g, b):
    te, Bs, D = ext_tm.shape
    ts = te - POOL_BUF
    bb = _tile(Bs, 16)
    gc = D // len(POOL_WINDOWS)
    vec = pl.BlockSpec((1, D), lambda bi: (0, 0))
    return pl.pallas_call(
        functools.partial(_pool_sample_kernel, start=start),
        grid=(Bs // bb,),
        in_specs=[pl.BlockSpec((te, bb, D), lambda bi: (0, bi, 0)),
                  pl.BlockSpec((len(POOL_WINDOWS), gc, gc), lambda bi: (0, 0, 0)),
                  vec, vec, vec],
        out_specs=pl.BlockSpec((ts, bb, D), lambda bi: (0, bi, 0)),
        out_shape=jax.ShapeDtypeStruct((ts, Bs, D), _f32),
        scratch_shapes=[pltpu.VMEM((bb, D), _f32)],
        compiler_params=_params(("parallel",)),
        name="pool_sample",
    )(ext_tm, w_bf, scale, g, b)


def _top_rows(s, n):
    rows = []
    for r in range(n):
        m = jnp.max(s, axis=0, keepdims=True)
        rows.append(m)
        if r + 1 < n:
            s = jnp.where(s == m, NEG_BIG, s)
    return rows


def _max_tree(xs):
    xs = list(xs)
    while len(xs) > 1:
        nxt = [jnp.maximum(xs[i], xs[i + 1]) for i in range(0, len(xs) - 1, 2)]
        if len(xs) % 2:
            nxt.append(xs[-1])
        xs = nxt
    return xs[0]


def _peer_stats_kernel(x_ref, wq_ref, keys_ref, s2_ref, thr_ref, c_ref):
    half = keys_ref.shape[2]
    q = _dot(x_ref[...].astype(_bf16), wq_ref[...])
    s1 = _dot_nt(keys_ref[0], q[:, :half], precision=lax.Precision.HIGHEST)
    s2 = _dot_nt(keys_ref[1], q[:, half:], precision=lax.Precision.HIGHEST)
    n = PEER_TOPK + 1
    t1 = _top_rows(s1, n)
    t2 = _top_rows(s2, n)
    cands = [t1[a] + t2[b] for a in range(n) for b in range(n) if (a + 1) * (b + 1) <= n]
    c0 = t1[0] + t2[0]
    z = jnp.zeros_like(c0)
    vals = []
    for r in range(n):
        m = _max_tree(cands)
        vals.append(m)
        if r < PEER_TOPK:
            z = z + jnp.exp(m - c0)
        if r + 1 < n:
            cands = [jnp.where(c == m, NEG_BIG, c) for c in cands]
    tau = 0.5 * (vals[PEER_TOPK - 1] + vals[PEER_TOPK])
    s2_ref[...] = s2 * LOG2E
    thr_ref[...] = (tau - s1) * LOG2E
    c_ref[...] = (s1 - (c0 + jnp.log(z))) * LOG2E


def _peer_stats(x, wq_bf, keys):
    N, D = x.shape
    n_keys, half = keys.shape[1], keys.shape[2]
    tn = _tile(N, 512)
    out = jax.ShapeDtypeStruct((PEER_HEADS, n_keys, N), _f32)
    ospec = pl.BlockSpec((None, n_keys, tn), lambda i, h: (h, 0, i))
    return pl.pallas_call(
        _peer_stats_kernel,
        grid=(N // tn, PEER_HEADS),
        in_specs=[pl.BlockSpec((tn, D), lambda i, h: (i, 0)),
                  pl.BlockSpec((D, 2 * half), lambda i, h: (0, h)),
                  pl.BlockSpec((2, n_keys, half), lambda i, h: (0, 0, 0))],
        out_specs=[ospec, ospec, ospec],
        out_shape=[out, out, out],
        compiler_params=_params(("parallel", "arbitrary")),
        name="peer_stats",
    )(x, wq_bf, keys)


def _gelu(h):
    return 0.5 * h * (1.0 + lax.erf(h * (1.0 / math.sqrt(2.0))))


def _peer_dense_kernel(x_hbm, u_ref, vt_ref, s2_ref, thr_ref, c_ref, g_ref, b_ref, o_ref,
                       xt_ref, acc_ref, ht_ref, at_ref, bc_ref, sem):
    i = pl.program_id(0)
    j = pl.program_id(1)
    n_keys = s2_ref.shape[1]
    te = u_ref.shape[0]
    tn, d = o_ref.shape
    sub = te // n_keys
    drow = d // sub
    slot = j % 2

    @pl.when(j == 0)
    def _():
        cp = pltpu.make_async_copy(x_hbm.at[pl.ds(pl.multiple_of(i * tn, tn), tn), :], o_ref, sem)
        cp.start()
        cp.wait()
        for cb in range(0, d, V7X_LANES):
            xt_ref[cb:cb + V7X_LANES, :] = o_ref[:, cb:cb + V7X_LANES].T.astype(_bf16)
        acc_ref[...] = jnp.zeros_like(acc_ref)
        at_ref[1] = jnp.zeros((te, tn), _bf16)

    ht_ref[...] = _dot(u_ref[...], xt_ref[...])
    a_prev = at_ref[1 - slot]
    for ii in range(sub):
        drows = slice(ii * drow, (ii + 1) * drow)
        acc_ref[drows, :] += _dot(vt_ref[drows, :], a_prev)
        i1 = jnp.minimum(j * sub + ii, n_keys - 1)
        for h in range(PEER_HEADS):
            bc_ref[ii, h] = jnp.broadcast_to(thr_ref[h, pl.ds(i1, 1), :], (V7X_SUBLANES, tn))
            bc_ref[ii, PEER_HEADS + h] = jnp.broadcast_to(c_ref[h, pl.ds(i1, 1), :],
                                                          (V7X_SUBLANES, tn))
        for r0 in range(0, n_keys, PEER_GATE_ROWS):
            gates = [None] * (PEER_GATE_ROWS // V7X_SUBLANES)
            for h in range(PEER_HEADS):
                thr = bc_ref[ii, h]
                c = bc_ref[ii, PEER_HEADS + h]
                for q in range(len(gates)):
                    s2 = s2_ref[h, r0 + q * V7X_SUBLANES:r0 + (q + 1) * V7X_SUBLANES, :]
                    term = jnp.where(s2 > thr, jnp.exp2(s2 + c), 0.0)
                    gates[q] = term if gates[q] is None else gates[q] + term
            gate = jnp.concatenate(gates, axis=0)
            rows = slice(ii * n_keys + r0, ii * n_keys + r0 + PEER_GATE_ROWS)
            at_ref[slot, rows, :] = (_gelu(ht_ref[rows, :]) * gate).astype(_bf16)

    @pl.when(j == pl.num_programs(1) - 1)
    def _():
        for cb in range(0, d, V7X_LANES):
            cs = slice(cb, cb + V7X_LANES)
            o_ref[:, cs] = ALPHA * o_ref[:, cs] + acc_ref[cs, :].T
        _residual_layer_norm(o_ref, None, g_ref, b_ref)


def _peer_dense(x, u_bf, vt_bf, s2t, thrt, ct, g, b):
    N, D = x.shape
    E = u_bf.shape[0]
    n_keys = s2t.shape[1]
    tn = _tile(N, 512)
    te = _tile(E, 4 * n_keys)
    nj = E // te
    vec = pl.BlockSpec((1, D), lambda i, j: (0, 0))
    once = pl.Buffered(1)
    stat = pl.BlockSpec((PEER_HEADS, n_keys, tn), lambda i, j: (0, 0, i), pipeline_mode=once)
    return pl.pallas_call(
        _peer_dense_kernel,
        grid=(N // tn, nj + 1),
        in_specs=[pl.BlockSpec(memory_space=pl.ANY),
                  pl.BlockSpec((te, D), lambda i, j: (jnp.minimum(j, nj - 1), 0)),
                  pl.BlockSpec((D, te), lambda i, j: (0, jnp.maximum(j - 1, 0))),
                  stat, stat, stat, vec, vec],
        out_specs=pl.BlockSpec((tn, D), lambda i, j: (i, 0)),
        out_shape=jax.ShapeDtypeStruct((N, D), _f32),
        scratch_shapes=[pltpu.VMEM((D, tn), _bf16), pltpu.VMEM((D, tn), _f32),
                        pltpu.VMEM((te, tn), _f32), pltpu.VMEM((2, te, tn), _bf16),
                        pltpu.VMEM((te // n_keys, 2 * PEER_HEADS, V7X_SUBLANES, tn), _f32),
                        pltpu.SemaphoreType.DMA(())],
        compiler_params=_params(("parallel", "arbitrary")),
        name="peer_dense",
    )(x, u_bf, vt_bf, s2t, thrt, ct, g, b)


def _peer_layer(x, wq_bf, keys, u_bf, vt_bf, g, b):
    s2t, thrt, ct = _peer_stats(x, wq_bf, keys)
    return _peer_dense(x, u_bf, vt_bf, s2t, thrt, ct, g, b)


def _rwkv_mix_kernel(x_ref, xp_ref, mu_ref, w1_ref, a1_ref, g1_ref,
                     xr_ref, xk_ref, xv_ref, hw_ref, ha_ref, hg_ref):
    x = x_ref[...]
    xx = xp_ref[...] - x

    def mix(i):
        return (x + xx * mu_ref[i:i + 1, :]).astype(_bf16)

    xr_ref[...] = mix(0)
    xk_ref[...] = mix(2)
    xv_ref[...] = mix(3)
    hw_ref[...] = jnp.tanh(_dot(mix(1), w1_ref[...])).astype(_bf16)
    ha_ref[...] = _dot(mix(4), a1_ref[...]).astype(_bf16)
    hg_ref[...] = (1.0 / (1.0 + jnp.exp(-_dot(mix(5), g1_ref[...])))).astype(_bf16)


def _rwkv_mix(x, xp, mu, w1_bf, a1_bf, g1_bf):
    N, D = x.shape
    tm = _tile(N, 256)
    row = pl.BlockSpec((tm, D), lambda i: (i, 0))

    def full(a):
        return pl.BlockSpec(a.shape, lambda i: (0, 0))

    def out(cols, dt):
        return pl.BlockSpec((tm, cols), lambda i: (i, 0)), jax.ShapeDtypeStruct((N, cols), dt)

    outs = [out(D, _bf16), out(D, _bf16), out(D, _bf16), out(w1_bf.shape[1], _bf16),
            out(a1_bf.shape[1], _bf16), out(g1_bf.shape[1], _bf16)]
    return pl.pallas_call(
        _rwkv_mix_kernel,
        grid=(N // tm,),
        in_specs=[row, row, full(mu), full(w1_bf), full(a1_bf), full(g1_bf)],
        out_specs=[o[0] for o in outs],
        out_shape=[o[1] for o in outs],
        compiler_params=_params(("parallel",)),
        name="rwkv_mix",
    )(x, xp, mu, w1_bf, a1_bf, g1_bf)


def _rwkv_proj_kernel(xr_ref, xk_ref, xv_ref, wr_ref, wk_ref, wv_ref, hw_ref, ha_ref, hg_ref,
                      w2_ref, a2_ref, g2_ref, w0_ref, a0_ref,
                      r_ref, lw_ref, k_ref, v_ref, a_ref, g_ref):
    r_ref[...] = _dot(xr_ref[...], wr_ref[...])
    k_ref[...] = _dot(xk_ref[...], wk_ref[...])
    v_ref[...] = _dot(xv_ref[...], wv_ref[...])
    u = -(w0_ref[...] + _dot(hw_ref[...], w2_ref[...]))
    softplus = jnp.maximum(u, 0.0) + jnp.log(1.0 + jnp.exp(-jnp.abs(u)))
    lw_ref[...] = -jnp.exp(-softplus - 0.5)
    a_ref[...] = 1.0 / (1.0 + jnp.exp(-(a0_ref[...] + _dot(ha_ref[...], a2_ref[...]))))
    g_ref[...] = _dot(hg_ref[...], g2_ref[...])


def _rwkv_proj(xr, xk, xv, wrkv_bf, hw, ha, hg, w2_bf, a2_bf, g2_bf, w0, a0):
    N, D = xr.shape
    tm = _tile(N, 512)
    tc = _tile(D, 256)
    row = pl.BlockSpec((tm, D), lambda i, j: (i, 0))

    def wmat(m):
        return pl.BlockSpec((None, D, tc), lambda i, j, m=m: (m, 0, j))

    def rowfull(a):
        return pl.BlockSpec((tm, a.shape[1]), lambda i, j: (i, 0))

    def colblk(a):
        return pl.BlockSpec((a.shape[0], tc), lambda i, j: (0, j))

    ospec = pl.BlockSpec((tm, tc), lambda i, j: (i, j))
    oshape = jax.ShapeDtypeStruct((N, D), _f32)
    return pl.pallas_call(
        _rwkv_proj_kernel,
        grid=(N // tm, D // tc),
        in_specs=[row, row, row, wmat(0), wmat(1), wmat(2), rowfull(hw), rowfull(ha), rowfull(hg),
                  colblk(w2_bf), colblk(a2_bf), colblk(g2_bf), colblk(w0), colblk(a0)],
        out_specs=[ospec] * 6,
        out_shape=[oshape] * 6,
        compiler_params=_params(("parallel", "arbitrary")),
        name="rwkv_proj",
    )(xr, xk, xv, wrkv_bf, wrkv_bf, wrkv_bf, hw, ha, hg, w2_bf, a2_bf, g2_bf, w0, a0)


def _wkv_kernel(r_ref, lw_ref, k_ref, v_ref, a_ref, s0_ref, kk_ref, ka_ref, rk_ref, gg_ref, gb_ref,
                y_ref, s_ref):
    tc = r_ref.shape[0]
    hd = r_ref.shape[1]

    @pl.when(pl.program_id(1) == 0)
    def _():
        s_ref[...] = s0_ref[...]

    def step(t, carry):
        r = r_ref[t]
        k = k_ref[t]
        a = a_ref[t]
        v = v_ref[t]
        dec = jnp.exp(lw_ref[t])
        kk = k * kk_ref[...]
        kk = kk * lax.rsqrt(jnp.maximum(jnp.sum(kk * kk, axis=0, keepdims=True), 1e-24))
        kmod = k * (1.0 + (a - 1.0) * ka_ref[...])
        an = -kk
        bn = kk * a
        dr = dec * r
        br = jnp.sum(bn * r, axis=0, keepdims=True)
        kr = jnp.sum(kmod * r, axis=0, keepdims=True)
        sa = jnp.zeros_like(v)
        z = jnp.zeros_like(v)
        for j in range(hd):
            sj = s_ref[j]
            sa = sa + sj * an[j:j + 1, :]
            z = z + sj * dr[j:j + 1, :]
        y = z + sa * br + v * kr
        for j in range(hd):
            s_ref[j] = s_ref[j] * dec[j:j + 1, :] + sa * bn[j:j + 1, :] + v * kmod[j:j + 1, :]
        m = jnp.mean(y, axis=0, keepdims=True)
        yc = y - m
        var = jnp.mean(yc * yc, axis=0, keepdims=True)
        yn = yc * lax.rsqrt(var + GN_EPS) * gg_ref[...] + gb_ref[...]
        bonus = jnp.sum(r * kmod * rk_ref[...], axis=0, keepdims=True)
        y_ref[t] = yn + bonus * v
        return carry

    lax.fori_loop(0, tc, step, 0)


def _wkv(r, lw, k, v, a, s0, kk_t, ka_t, rk_t, gg_t, gb_t):
    T, hd, L = r.shape
    tl = _tile(L, V7X_LANES)
    tc = _tile(T, 32)
    op = pl.BlockSpec((tc, hd, tl), lambda g, c: (c, 0, g))
    st = pl.BlockSpec((hd, hd, tl), lambda g, c: (0, 0, g))
    par = pl.BlockSpec((hd, tl), lambda g, c: (0, g))
    return pl.pallas_call(
        _wkv_kernel,
        grid=(L // tl, T // tc),
        in_specs=[op, op, op, op, op, st, par, par, par, par, par],
        out_specs=[op, st],
        out_shape=[jax.ShapeDtypeStruct((T, hd, L), _f32), jax.ShapeDtypeStruct((hd, hd, L), _f32)],
        compiler_params=_params(("parallel", "arbitrary")),
        name="wkv_scan",
    )(r, lw, k, v, a, s0, kk_t, ka_t, rk_t, gg_t, gb_t)


def _rwkv_out_kernel(y_ref, g_ref, wo_ref, x_ref, lg_ref, lb_ref, o_ref):
    kt = pl.program_id(1)

    @pl.when(kt == 0)
    def _():
        o_ref[...] = jnp.zeros_like(o_ref)

    o_ref[...] += _dot((y_ref[...] * g_ref[...]).astype(_bf16), wo_ref[...])

    @pl.when(kt == pl.num_programs(1) - 1)
    def _():
        _residual_layer_norm(o_ref, x_ref, lg_ref, lb_ref)


def _rwkv_out(y, g, wo_bf, x, lg, lb):
    N, D = x.shape
    tm = _tile(N, 512)
    tk = _tile(D, 512)
    vec = pl.BlockSpec((1, D), lambda i, kt: (0, 0))
    return pl.pallas_call(
        _rwkv_out_kernel,
        grid=(N // tm, D // tk),
        in_specs=[pl.BlockSpec((tm, tk), lambda i, kt: (i, kt)),
                  pl.BlockSpec((tm, tk), lambda i, kt: (i, kt)),
                  pl.BlockSpec((tk, D), lambda i, kt: (kt, 0)),
                  pl.BlockSpec((tm, D), lambda i, kt: (i, 0)),
                  vec, vec],
        out_specs=pl.BlockSpec((tm, D), lambda i, kt: (i, 0)),
        out_shape=jax.ShapeDtypeStruct((N, D), _f32),
        compiler_params=_params(("parallel", "arbitrary")),
        name="rwkv_out",
    )(y, g, wo_bf, x, lg, lb)


def _to_lanes(a, lead, hd):
    T, B = lead
    H = a.shape[-1] // hd
    return a.reshape(T, B, H, hd).transpose(0, 3, 1, 2).reshape(T, hd, B * H)


def _from_lanes(a, lead, hd):
    T, B = lead
    H = a.shape[-1] // B
    return a.reshape(T, hd, B, H).transpose(0, 2, 3, 1).reshape(T, B, H * hd)


def _head_param(p, B, hd):
    H = p.shape[-1] // hd
    return jnp.tile(p.reshape(H, hd).T[:, None, :], (1, B, 1)).reshape(hd, B * H)


def kernel(x_prompt, x_sample, state_pool, state_rwkv_shift, state_rwkv_wkv, pool_w, pool_scale,
           rwkv_mu, rwkv_w_rkv, rwkv_w0, rwkv_w1, rwkv_w2, rwkv_a0, rwkv_a1, rwkv_a2, rwkv_g1,
           rwkv_g2, rwkv_k_k, rwkv_k_a, rwkv_r_k, rwkv_lnx_g, rwkv_lnx_b, rwkv_w_o, peer_w_q,
           peer_keys, peer_u, peer_v, ln_g, ln_b):
    B, T, D = x_prompt.shape
    Bs, Ts, _ = x_sample.shape
    hd = RWKV_HEAD
    H = D // hd
    n_p = B * T
    bf = lambda a: a.astype(_bf16)
    row = lambda a: a.reshape(1, D)

    pw = bf(pool_w[0])
    ps = row(pool_scale[0])
    g00, b00 = row(ln_g[0, 0]), row(ln_b[0, 0])
    past_p = jnp.zeros((B, POOL_BUF, D), x_prompt.dtype)
    x1p = _pool_prompt(x_prompt, past_p, 0, pw, ps, g00, b00)
    ext_s = jnp.concatenate([state_pool[0], x_sample], axis=1)
    x1s = _pool_sample(ext_s.transpose(1, 0, 2), PAST_LEN, pw, ps, g00, b00)
    x1 = jnp.concatenate([x1p.reshape(n_p, D), x1s.reshape(Ts * Bs, D)], axis=0)
    pool_p = x_prompt[:, T - POOL_BUF:][None]
    pool_s = ext_s[:, Ts:][None]

    x2 = _peer_layer(x1, bf(peer_w_q[0]), peer_keys[0], bf(peer_u[0]), bf(peer_v[0].T),
                     row(ln_g[0, 1]), row(ln_b[0, 1]))

    x2p = x2[:n_p].reshape(B, T, D)
    x2s = x2[n_p:].reshape(Ts, Bs, D)
    xp_p = jnp.concatenate([jnp.zeros((B, 1, D), _f32), x2p[:, :-1]], axis=1)
    xp_s = jnp.concatenate([state_rwkv_shift[0][None], x2s[:-1]], axis=0)
    xprev = jnp.concatenate([xp_p.reshape(n_p, D), xp_s.reshape(Ts * Bs, D)], axis=0)
    shift_p = x2p[:, -1][None]
    shift_s = x2s[-1][None]

    g1 = rwkv_g1[0]
    g2 = rwkv_g2[0]
    gpad = (-g1.shape[1]) % V7X_LANES
    g1 = jnp.pad(g1, ((0, 0), (0, gpad)))
    g2 = jnp.pad(g2, ((0, gpad), (0, 0)))
    xr, xk, xv, hw, ha, hg = _rwkv_mix(x2, xprev, rwkv_mu[0], bf(rwkv_w1[0]), bf(rwkv_a1[0]), bf(g1))
    r, lw, k, v, a, g = _rwkv_proj(xr, xk, xv, bf(rwkv_w_rkv[0]), hw, ha, hg, bf(rwkv_w2[0]),
                                   bf(rwkv_a2[0]), bf(g2), row(rwkv_w0[0]), row(rwkv_a0[0]))

    def scan(time_major, s0_t):
        ops = [time_major(o) for o in (r, lw, k, v, a)]
        nt, nb = ops[0].shape[:2]
        ops = [_to_lanes(o, (nt, nb), hd) for o in ops]
        pars = [_head_param(p, nb, hd) for p in (rwkv_k_k[0], rwkv_k_a[0], rwkv_r_k[0].reshape(D),
                                                 rwkv_lnx_g[0], rwkv_lnx_b[0])]
        y, s_fin = _wkv(*ops, s0_t, *pars)
        y = _from_lanes(y, (nt, nb), hd)
        s_fin = s_fin.reshape(hd, hd, nb, H).transpose(2, 3, 1, 0)
        return y, s_fin

    y_p, wkv_p = scan(lambda o: o[:n_p].reshape(B, T, D).transpose(1, 0, 2),
                      jnp.zeros((hd, hd, B * H), _f32))
    s0_s = state_rwkv_wkv[0].transpose(3, 2, 0, 1).reshape(hd, hd, Bs * H)
    y_s, wkv_s = scan(lambda o: o[n_p:].reshape(Ts, Bs, D), s0_s)
    y = jnp.concatenate([y_p.transpose(1, 0, 2).reshape(n_p, D), y_s.reshape(Ts * Bs, D)], axis=0)
    x3 = _rwkv_out(y, g, bf(rwkv_w_o[0]), x2, row(ln_g[1, 0]), row(ln_b[1, 0]))

    x4 = _peer_layer(x3, bf(peer_w_q[1]), peer_keys[1], bf(peer_u[1]), bf(peer_v[1].T),
                     row(ln_g[1, 1]), row(ln_b[1, 1]))
    y_prompt = x4[:n_p].reshape(B, T, D)
    y_sample = x4[n_p:].reshape(Ts, Bs, D).transpose(1, 0, 2)
    return (y_prompt, y_sample, pool_p, pool_s, shift_p, shift_s, wkv_p[None], wkv_s[None])
```

```python
import functools
import math

import jax
import jax.numpy as jnp
from jax import lax
from jax.experimental import pallas as pl
from jax.experimental.pallas import tpu as pltpu

POOL_WINDOWS = (2, 4, 8, 16)
POOL_BUF = max(POOL_WINDOWS) - 1
POOL_HALO = POOL_BUF + 1
RWKV_HEAD = 64
GN_EPS = 64e-5
PEER_HEADS = 8
PEER_TOPK = 16
PEER_GATE_ROWS = 16
LN_EPS = 1e-5
DEPTH = 2
ALPHA = (2.0 * DEPTH) ** 0.25
PAST_LEN = 16384

V7X_LANES = 128
V7X_SUBLANES = 8
LN_ROWS = 64
WKV_PREP_STEPS = 8
WKV_KEY_UNROLL = 8
V7X_VMEM_LIMIT = 63 * 1024 * 1024
NEG_BIG = -3.0e38
LOG2E = 1.4426950408889634

_f32 = jnp.float32
_bf16 = jnp.bfloat16


def _tile(dim, pref, align=8):
    if dim <= pref:
        return dim
    for t in range(pref - pref % align, 0, -align):
        if dim % t == 0:
            return t
    raise ValueError((dim, pref, align))


def _params(sem):
    return pltpu.CompilerParams(dimension_semantics=sem, vmem_limit_bytes=V7X_VMEM_LIMIT)


def _layer_norm(z, g, b):
    mu = jnp.mean(z, axis=-1, keepdims=True)
    zc = z - mu
    var = jnp.mean(zc * zc, axis=-1, keepdims=True)
    return zc * lax.rsqrt(var + LN_EPS) * g + b


def _residual_layer_norm(o_ref, x_ref, g_ref, b_ref):
    n_rows = o_ref.shape[0]
    rows_per = LN_ROWS if n_rows % LN_ROWS == 0 else n_rows
    g = g_ref[...]
    b = b_ref[...]

    def body(i, carry):
        rows = pl.ds(pl.multiple_of(i * rows_per, rows_per), rows_per)
        z = o_ref[rows, :]
        if x_ref is not None:
            z = ALPHA * x_ref[rows, :] + z
        o_ref[rows, :] = _layer_norm(z, g, b)
        return carry

    lax.fori_loop(0, n_rows // rows_per, body, 0)


def _dot(a, b, precision=None):
    return jnp.dot(a, b, preferred_element_type=_f32, precision=precision)


def _dot_nt(a, b, precision=None):
    return lax.dot_general(a, b, (((1,), (1,)), ((), ())), preferred_element_type=_f32,
                           precision=precision)


def _dot_tn(a, b, precision=None):
    return lax.dot_general(a, b, (((0,), (0,)), ((), ())), preferred_element_type=_f32,
                           precision=precision)


def _pool_finish(x, window_sum, n_real, w_ref, scale_ref, g_ref, b_ref, z_ref):
    gc = x.shape[-1] // len(POOL_WINDOWS)
    for gi, w in enumerate(POOL_WINDOWS):
        cs = slice(gi * gc, (gi + 1) * gc)
        if isinstance(n_real, int):
            inv_cnt = 1.0 / float(min(n_real, w))
        else:
            inv_cnt = 1.0 / jnp.minimum(n_real, w).astype(_f32)
        d = window_sum(gi, w) * inv_cnt - x[:, cs]
        y = _dot(d.astype(_bf16), w_ref[gi]) * scale_ref[:, cs]
        z_ref[:, cs] = ALPHA * x[:, cs] + y
    _residual_layer_norm(z_ref, None, g_ref, b_ref)


def _pool_prompt_kernel(x_ref, halo_ref, w_ref, scale_ref, g_ref, b_ref, o_ref, ext_ref, *, start):
    tt = x_ref.shape[0]
    gc = x_ref.shape[1] // len(POOL_WINDOWS)
    ext_ref[0:POOL_HALO, :] = halo_ref[...]
    ext_ref[POOL_HALO:POOL_HALO + tt, :] = x_ref[...]
    t0 = pl.program_id(1) * tt
    n_real = lax.broadcasted_iota(jnp.int32, (tt, 1), 0) + (t0 + start + 1)

    def window_sum(gi, w):
        cs = slice(gi * gc, (gi + 1) * gc)
        s = ext_ref[POOL_HALO:POOL_HALO + tt, cs]
        for k in range(1, w):
            s = s + ext_ref[POOL_HALO - k:POOL_HALO - k + tt, cs]
        return s

    _pool_finish(x_ref[...], window_sum, n_real, w_ref, scale_ref, g_ref, b_ref, o_ref)


def _pool_prompt(x, past, start, w_bf, scale, g, b):
    B, T, D = x.shape
    tt = _tile(T, 256)
    nt = T // tt
    gc = D // len(POOL_WINDOWS)
    xe = jnp.concatenate([jnp.zeros((B, 1, D), x.dtype), past, x], axis=1)[:, :T]
    halo = xe.reshape(B, nt, tt, D)[:, :, :POOL_HALO]
    vec = pl.BlockSpec((1, D), lambda bi, ti: (0, 0))
    return pl.pallas_call(
        functools.partial(_pool_prompt_kernel, start=start),
        grid=(B, nt),
        in_specs=[pl.BlockSpec((None, tt, D), lambda bi, ti: (bi, ti, 0)),
                  pl.BlockSpec((None, None, POOL_HALO, D), lambda bi, ti: (bi, ti, 0, 0)),
                  pl.BlockSpec((len(POOL_WINDOWS), gc, gc), lambda bi, ti: (0, 0, 0)),
                  vec, vec, vec],
        out_specs=pl.BlockSpec((None, tt, D), lambda bi, ti: (bi, ti, 0)),
        out_shape=jax.ShapeDtypeStruct((B, T, D), _f32),
        scratch_shapes=[pltpu.VMEM((POOL_HALO + tt, D), _f32)],
        compiler_params=_params(("parallel", "arbitrary")),
        name="pool_prompt",
    )(x, halo, w_bf, scale, g, b)


def _pool_sample_kernel(ext_ref, w_ref, scale_ref, g_ref, b_ref, o_ref, z_ref, *, start):
    ts = o_ref.shape[0]
    gc = o_ref.shape[2] // len(POOL_WINDOWS)
    for t in range(ts):
        def window_sum(gi, w, t=t):
            cs = slice(gi * gc, (gi + 1) * gc)
            s = ext_ref[POOL_BUF + t, :, cs]
            for k in range(1, w):
                s = s + ext_ref[POOL_BUF + t - k, :, cs]
            return s

        _pool_finish(ext_ref[POOL_BUF + t], window_sum, start + t + 1, w_ref, scale_ref,
                     g_ref, b_ref, z_ref)
        o_ref[t] = z_ref[...]


def _pool_sample(ext_tm, start, w_bf, scale, g, b):
    te, Bs, D = ext_tm.shape
    ts = te - POOL_BUF
    bb = _tile(Bs, 16)
    gc = D // len(POOL_WINDOWS)
    vec = pl.BlockSpec((1, D), lambda bi: (0, 0))
    return pl.pallas_call(
        functools.partial(_pool_sample_kernel, start=start),
        grid=(Bs // bb,),
        in_specs=[pl.BlockSpec((te, bb, D), lambda bi: (0, bi, 0)),
                  pl.BlockSpec((len(POOL_WINDOWS), gc, gc), lambda bi: (0, 0, 0)),
                  vec, vec, vec],
        out_specs=pl.BlockSpec((ts, bb, D), lambda bi: (0, bi, 0)),
        out_shape=jax.ShapeDtypeStruct((ts, Bs, D), _f32),
        scratch_shapes=[pltpu.VMEM((bb, D), _f32)],
        compiler_params=_params(("parallel",)),
        name="pool_sample",
    )(ext_tm, w_bf, scale, g, b)


def _top_rows(s, n):
    rows = []
    for r in range(n):
        m = jnp.max(s, axis=0, keepdims=True)
        rows.append(m)
        if r + 1 < n:
            s = jnp.where(s == m, NEG_BIG, s)
    return rows


def _max_tree(xs):
    xs = list(xs)
    while len(xs) > 1:
        nxt = [jnp.maximum(xs[i], xs[i + 1]) for i in range(0, len(xs) - 1, 2)]
        if len(xs) % 2:
            nxt.append(xs[-1])
        xs = nxt
    return xs[0]


def _peer_stats_kernel(x_ref, wq_ref, keys_ref, s2_ref, thr_ref, c_ref):
    half = keys_ref.shape[2]
    q = _dot(x_ref[...].astype(_bf16), wq_ref[...])
    s1 = _dot_nt(keys_ref[0], q[:, :half], precision=lax.Precision.HIGHEST)
    s2 = _dot_nt(keys_ref[1], q[:, half:], precision=lax.Precision.HIGHEST)
    n = PEER_TOPK + 1
    t1 = _top_rows(s1, n)
    t2 = _top_rows(s2, n)
    cands = [t1[a] + t2[b] for a in range(n) for b in range(n) if (a + 1) * (b + 1) <= n]
    c0 = t1[0] + t2[0]
    z = jnp.zeros_like(c0)
    vals = []
    for r in range(n):
        m = _max_tree(cands)
        vals.append(m)
        if r < PEER_TOPK:
            z = z + jnp.exp(m - c0)
        if r + 1 < n:
            cands = [jnp.where(c == m, NEG_BIG, c) for c in cands]
    tau = 0.5 * (vals[PEER_TOPK - 1] + vals[PEER_TOPK])
    s2_ref[...] = s2 * LOG2E
    thr_ref[...] = (tau - s1) * LOG2E
    c_ref[...] = (s1 - (c0 + jnp.log(z))) * LOG2E


def _peer_stats(x, wq_bf, keys):
    N, D = x.shape
    n_keys, half = keys.shape[1], keys.shape[2]
    tn = _tile(N, 512)
    out = jax.ShapeDtypeStruct((PEER_HEADS, n_keys, N), _f32)
    ospec = pl.BlockSpec((None, n_keys, tn), lambda i, h: (h, 0, i))
    return pl.pallas_call(
        _peer_stats_kernel,
        grid=(N // tn, PEER_HEADS),
        in_specs=[pl.BlockSpec((tn, D), lambda i, h: (i, 0)),
                  pl.BlockSpec((D, 2 * half), lambda i, h: (0, h)),
                  pl.BlockSpec((2, n_keys, half), lambda i, h: (0, 0, 0))],
        out_specs=[ospec, ospec, ospec],
        out_shape=[out, out, out],
        compiler_params=_params(("parallel", "arbitrary")),
        name="peer_stats",
    )(x, wq_bf, keys)


def _gelu(h):
    return 0.5 * h * (1.0 + lax.erf(h * (1.0 / math.sqrt(2.0))))


def _peer_dense_kernel(x_hbm, u_ref, vt_ref, s2_ref, thr_ref, c_ref, g_ref, b_ref, o_ref,
                       xt_ref, acc_ref, ht_ref, at_ref, bc_ref, sem):
    i = pl.program_id(0)
    j = pl.program_id(1)
    n_keys = s2_ref.shape[1]
    te = u_ref.shape[0]
    tn, d = o_ref.shape
    sub = te // n_keys
    drow = d // sub
    slot = j % 2

    @pl.when(j == 0)
    def _():
        cp = pltpu.make_async_copy(x_hbm.at[pl.ds(pl.multiple_of(i * tn, tn), tn), :], o_ref, sem)
        cp.start()
        cp.wait()
        for cb in range(0, d, V7X_LANES):
            xt_ref[cb:cb + V7X_LANES, :] = o_ref[:, cb:cb + V7X_LANES].T.astype(_bf16)
        acc_ref[...] = jnp.zeros_like(acc_ref)
        at_ref[1] = jnp.zeros((te, tn), _bf16)

    ht_ref[...] = _dot(u_ref[...], xt_ref[...])
    a_prev = at_ref[1 - slot]
    for ii in range(sub):
        drows = slice(ii * drow, (ii + 1) * drow)
        acc_ref[drows, :] += _dot(vt_ref[drows, :], a_prev)
        i1 = jnp.minimum(j * sub + ii, n_keys - 1)
        for h in range(PEER_HEADS):
            bc_ref[ii, h] = jnp.broadcast_to(thr_ref[h, pl.ds(i1, 1), :], (V7X_SUBLANES, tn))
            bc_ref[ii, PEER_HEADS + h] = jnp.broadcast_to(c_ref[h, pl.ds(i1, 1), :],
                                                          (V7X_SUBLANES, tn))
        for r0 in range(0, n_keys, PEER_GATE_ROWS):
            gates = [None] * (PEER_GATE_ROWS // V7X_SUBLANES)
            for h in range(PEER_HEADS):
                thr = bc_ref[ii, h]
                c = bc_ref[ii, PEER_HEADS + h]
                for q in range(len(gates)):
                    s2 = s2_ref[h, r0 + q * V7X_SUBLANES:r0 + (q + 1) * V7X_SUBLANES, :]
                    term = jnp.where(s2 > thr, jnp.exp2(s2 + c), 0.0)
                    gates[q] = term if gates[q] is None else gates[q] + term
            gate = jnp.concatenate(gates, axis=0)
            rows = slice(ii * n_keys + r0, ii * n_keys + r0 + PEER_GATE_ROWS)
            at_ref[slot, rows, :] = (_gelu(ht_ref[rows, :]) * gate).astype(_bf16)

    @pl.when(j == pl.num_programs(1) - 1)
    def _():
        for cb in range(0, d, V7X_LANES):
            cs = slice(cb, cb + V7X_LANES)
            o_ref[:, cs] = ALPHA * o_ref[:, cs] + acc_ref[cs, :].T
        _residual_layer_norm(o_ref, None, g_ref, b_ref)


def _peer_dense(x, u_bf, vt_bf, s2t, thrt, ct, g, b):
    N, D = x.shape
    E = u_bf.shape[0]
    n_keys = s2t.shape[1]
    tn = _tile(N, 512)
    te = _tile(E, 4 * n_keys)
    nj = E // te
    vec = pl.BlockSpec((1, D), lambda i, j: (0, 0))
    once = pl.Buffered(1)
    stat = pl.BlockSpec((PEER_HEADS, n_keys, tn), lambda i, j: (0, 0, i), pipeline_mode=once)
    return pl.pallas_call(
        _peer_dense_kernel,
        grid=(N // tn, nj + 1),
        in_specs=[pl.BlockSpec(memory_space=pl.ANY),
                  pl.BlockSpec((te, D), lambda i, j: (jnp.minimum(j, nj - 1), 0)),
                  pl.BlockSpec((D, te), lambda i, j: (0, jnp.maximum(j - 1, 0))),
                  stat, stat, stat, vec, vec],
        out_specs=pl.BlockSpec((tn, D), lambda i, j: (i, 0)),
        out_shape=jax.ShapeDtypeStruct((N, D), _f32),
        scratch_shapes=[pltpu.VMEM((D, tn), _bf16), pltpu.VMEM((D, tn), _f32),
                        pltpu.VMEM((te, tn), _f32), pltpu.VMEM((2, te, tn), _bf16),
                        pltpu.VMEM((te // n_keys, 2 * PEER_HEADS, V7X_SUBLANES, tn), _f32),
                        pltpu.SemaphoreType.DMA(())],
        compiler_params=_params(("parallel", "arbitrary")),
        name="peer_dense",
    )(x, u_bf, vt_bf, s2t, thrt, ct, g, b)


def _peer_layer(x, wq_bf, keys, u_bf, vt_bf, g, b):
    s2t, thrt, ct = _peer_stats(x, wq_bf, keys)
    return _peer_dense(x, u_bf, vt_bf, s2t, thrt, ct, g, b)


def _rwkv_mix_kernel(x_ref, xp_ref, mu_ref, w1_ref, a1_ref, g1_ref,
                     xr_ref, xk_ref, xv_ref, hw_ref, ha_ref, hg_ref):
    x = x_ref[...]
    xx = xp_ref[...] - x

    def mix(i):
        return (x + xx * mu_ref[i:i + 1, :]).astype(_bf16)

    xr_ref[...] = mix(0)
    xk_ref[...] = mix(2)
    xv_ref[...] = mix(3)
    hw_ref[...] = jnp.tanh(_dot(mix(1), w1_ref[...])).astype(_bf16)
    ha_ref[...] = _dot(mix(4), a1_ref[...]).astype(_bf16)
    hg_ref[...] = (1.0 / (1.0 + jnp.exp(-_dot(mix(5), g1_ref[...])))).astype(_bf16)


def _rwkv_mix(x, xp, mu, w1_bf, a1_bf, g1_bf):
    N, D = x.shape
    tm = _tile(N, 256)
    row = pl.BlockSpec((tm, D), lambda i: (i, 0))

    def full(a):
        return pl.BlockSpec(a.shape, lambda i: (0, 0))

    def out(cols, dt):
        return pl.BlockSpec((tm, cols), lambda i: (i, 0)), jax.ShapeDtypeStruct((N, cols), dt)

    outs = [out(D, _bf16), out(D, _bf16), out(D, _bf16), out(w1_bf.shape[1], _bf16),
            out(a1_bf.shape[1], _bf16), out(g1_bf.shape[1], _bf16)]
    return pl.pallas_call(
        _rwkv_mix_kernel,
        grid=(N // tm,),
        in_specs=[row, row, full(mu), full(w1_bf), full(a1_bf), full(g1_bf)],
        out_specs=[o[0] for o in outs],
        out_shape=[o[1] for o in outs],
        compiler_params=_params(("parallel",)),
        name="rwkv_mix",
    )(x, xp, mu, w1_bf, a1_bf, g1_bf)


def _rwkv_proj_kernel(xr_ref, xk_ref, xv_ref, wr_ref, wk_ref, wv_ref, hw_ref, ha_ref, hg_ref,
                      w2_ref, a2_ref, g2_ref, w0_ref, a0_ref,
                      r_ref, lw_ref, k_ref, v_ref, a_ref, g_ref):
    r_ref[...] = _dot(xr_ref[...], wr_ref[...])
    k_ref[...] = _dot(xk_ref[...], wk_ref[...])
    v_ref[...] = _dot(xv_ref[...], wv_ref[...])
    u = -(w0_ref[...] + _dot(hw_ref[...], w2_ref[...]))
    softplus = jnp.maximum(u, 0.0) + jnp.log(1.0 + jnp.exp(-jnp.abs(u)))
    lw_ref[...] = -jnp.exp(-softplus - 0.5)
    a_ref[...] = 1.0 / (1.0 + jnp.exp(-(a0_ref[...] + _dot(ha_ref[...], a2_ref[...]))))
    g_ref[...] = _dot(hg_ref[...], g2_ref[...])


def _rwkv_proj(xr, xk, xv, wrkv_bf, hw, ha, hg, w2_bf, a2_bf, g2_bf, w0, a0):
    N, D = xr.shape
    tm = _tile(N, 512)
    tc = _tile(D, 256)
    row = pl.BlockSpec((tm, D), lambda i, j: (i, 0))

    def wmat(m):
        return pl.BlockSpec((None, D, tc), lambda i, j, m=m: (m, 0, j))

    def rowfull(a):
        return pl.BlockSpec((tm, a.shape[1]), lambda i, j: (i, 0))

    def colblk(a):
        return pl.BlockSpec((a.shape[0], tc), lambda i, j: (0, j))

    ospec = pl.BlockSpec((tm, tc), lambda i, j: (i, j))
    oshape = jax.ShapeDtypeStruct((N, D), _f32)
    return pl.pallas_call(
        _rwkv_proj_kernel,
        grid=(N // tm, D // tc),
        in_specs=[row, row, row, wmat(0), wmat(1), wmat(2), rowfull(hw), rowfull(ha), rowfull(hg),
                  colblk(w2_bf), colblk(a2_bf), colblk(g2_bf), colblk(w0), colblk(a0)],
        out_specs=[ospec] * 6,
        out_shape=[oshape] * 6,
        compiler_params=_params(("parallel", "arbitrary")),
        name="rwkv_proj",
    )(xr, xk, xv, wrkv_bf, wrkv_bf, wrkv_bf, hw, ha, hg, w2_bf, a2_bf, g2_bf, w0, a0)


def _wkv_kernel(r_ref, lw_ref, k_ref, v_ref, a_ref, s0_ref, kk_ref, ka_ref, rk_ref, gg_ref, gb_ref,
                y_ref, s_ref, an_ref, dr_ref, dec_ref, bn_ref, km_ref, sc_ref):
    tc = r_ref.shape[0]
    hd = r_ref.shape[1]
    ps = WKV_PREP_STEPS if tc % WKV_PREP_STEPS == 0 else tc

    @pl.when(pl.program_id(1) == 0)
    def _():
        s_ref[...] = s0_ref[...]

    def prep(i, carry):
        ts = pl.ds(pl.multiple_of(i * ps, ps), ps)
        r = r_ref[ts]
        k = k_ref[ts]
        a = a_ref[ts]
        dec = jnp.exp(lw_ref[ts])
        kk = k * kk_ref[...]
        kk = kk * lax.rsqrt(jnp.maximum(jnp.sum(kk * kk, axis=1, keepdims=True), 1e-24))
        kmod = k * (1.0 + (a - 1.0) * ka_ref[...])
        bn = kk * a
        an_ref[ts] = -kk
        dr_ref[ts] = dec * r
        dec_ref[ts] = dec
        bn_ref[ts] = bn
        km_ref[ts] = kmod
        sc_ref[0, ts, :] = jnp.sum(bn * r, axis=1)
        sc_ref[1, ts, :] = jnp.sum(kmod * r, axis=1)
        sc_ref[2, ts, :] = jnp.sum(r * kmod * rk_ref[...], axis=1)
        return carry

    lax.fori_loop(0, tc // ps, prep, 0)

    def step(t, carry):
        v = v_ref[t]
        row = pl.ds(t, 1)
        ju = WKV_KEY_UNROLL if hd % WKV_KEY_UNROLL == 0 else hd

        def reduce_keys(jc, acc):
            sa, z = acc
            for jj in range(ju):
                j = jc * ju + jj
                sj = s_ref[j]
                sa = sa + sj * an_ref[t, pl.ds(j, 1), :]
                z = z + sj * dr_ref[t, pl.ds(j, 1), :]
            return sa, z

        sa, z = lax.fori_loop(0, hd // ju, reduce_keys, (jnp.zeros_like(v), jnp.zeros_like(v)))
        y_ref[t] = z + sa * sc_ref[0, row, :] + v * sc_ref[1, row, :]

        def update_keys(jc, c2):
            for jj in range(ju):
                j = jc * ju + jj
                s_ref[j] = (s_ref[j] * dec_ref[t, pl.ds(j, 1), :] + sa * bn_ref[t, pl.ds(j, 1), :]
                            + v * km_ref[t, pl.ds(j, 1), :])
            return c2

        lax.fori_loop(0, hd // ju, update_keys, 0)
        return carry

    lax.fori_loop(0, tc, step, 0)

    def finish(i, carry):
        ts = pl.ds(pl.multiple_of(i * ps, ps), ps)
        y = y_ref[ts]
        m = jnp.mean(y, axis=1, keepdims=True)
        yc = y - m
        var = jnp.mean(yc * yc, axis=1, keepdims=True)
        yn = yc * lax.rsqrt(var + GN_EPS) * gg_ref[...] + gb_ref[...]
        y_ref[ts] = yn + sc_ref[2, ts, :][:, None, :] * v_ref[ts]
        return carry

    lax.fori_loop(0, tc // ps, finish, 0)


def _wkv(r, lw, k, v, a, s0, kk_t, ka_t, rk_t, gg_t, gb_t):
    T, hd, L = r.shape
    tl = _tile(L, V7X_LANES)
    tc = _tile(T, 32)
    op = pl.BlockSpec((tc, hd, tl), lambda g, c: (c, 0, g))
    st = pl.BlockSpec((hd, hd, tl), lambda g, c: (0, 0, g))
    par = pl.BlockSpec((hd, tl), lambda g, c: (0, g))
    return pl.pallas_call(
        _wkv_kernel,
        grid=(L // tl, T // tc),
        in_specs=[op, op, op, op, op, st, par, par, par, par, par],
        out_specs=[op, st],
        out_shape=[jax.ShapeDtypeStruct((T, hd, L), _f32), jax.ShapeDtypeStruct((hd, hd, L), _f32)],
        scratch_shapes=[pltpu.VMEM((tc, hd, tl), _f32)] * 5 + [pltpu.VMEM((3, tc, tl), _f32)],
        compiler_params=_params(("parallel", "arbitrary")),
        name="wkv_scan",
    )(r, lw, k, v, a, s0, kk_t, ka_t, rk_t, gg_t, gb_t)


def _rwkv_out_kernel(y_ref, g_ref, wo_ref, x_ref, lg_ref, lb_ref, o_ref):
    kt = pl.program_id(1)

    @pl.when(kt == 0)
    def _():
        o_ref[...] = jnp.zeros_like(o_ref)

    o_ref[...] += _dot((y_ref[...] * g_ref[...]).astype(_bf16), wo_ref[...])

    @pl.when(kt == pl.num_programs(1) - 1)
    def _():
        _residual_layer_norm(o_ref, x_ref, lg_ref, lb_ref)


def _rwkv_out(y, g, wo_bf, x, lg, lb):
    N, D = x.shape
    tm = _tile(N, 512)
    tk = _tile(D, 512)
    vec = pl.BlockSpec((1, D), lambda i, kt: (0, 0))
    return pl.pallas_call(
        _rwkv_out_kernel,
        grid=(N // tm, D // tk),
        in_specs=[pl.BlockSpec((tm, tk), lambda i, kt: (i, kt)),
                  pl.BlockSpec((tm, tk), lambda i, kt: (i, kt)),
                  pl.BlockSpec((tk, D), lambda i, kt: (kt, 0)),
                  pl.BlockSpec((tm, D), lambda i, kt: (i, 0)),
                  vec, vec],
        out_specs=pl.BlockSpec((tm, D), lambda i, kt: (i, 0)),
        out_shape=jax.ShapeDtypeStruct((N, D), _f32),
        compiler_params=_params(("parallel", "arbitrary")),
        name="rwkv_out",
    )(y, g, wo_bf, x, lg, lb)


def _to_lanes(a, lead, hd):
    T, B = lead
    H = a.shape[-1] // hd
    return a.reshape(T, B, H, hd).transpose(0, 3, 1, 2).reshape(T, hd, B * H)


def _from_lanes(a, lead, hd):
    T, B = lead
    H = a.shape[-1] // B
    return a.reshape(T, hd, B, H).transpose(0, 2, 3, 1).reshape(T, B, H * hd)


def _head_param(p, B, hd):
    H = p.shape[-1] // hd
    return jnp.tile(p.reshape(H, hd).T[:, None, :], (1, B, 1)).reshape(hd, B * H)


def kernel(x_prompt, x_sample, state_pool, state_rwkv_shift, state_rwkv_wkv, pool_w, pool_scale,
           rwkv_mu, rwkv_w_rkv, rwkv_w0, rwkv_w1, rwkv_w2, rwkv_a0, rwkv_a1, rwkv_a2, rwkv_g1,
           rwkv_g2, rwkv_k_k, rwkv_k_a, rwkv_r_k, rwkv_lnx_g, rwkv_lnx_b, rwkv_w_o, peer_w_q,
           peer_keys, peer_u, peer_v, ln_g, ln_b):
    B, T, D = x_prompt.shape
    Bs, Ts, _ = x_sample.shape
    hd = RWKV_HEAD
    H = D // hd
    n_p = B * T
    bf = lambda a: a.astype(_bf16)
    row = lambda a: a.reshape(1, D)

    pw = bf(pool_w[0])
    ps = row(pool_scale[0])
    g00, b00 = row(ln_g[0, 0]), row(ln_b[0, 0])
    past_p = jnp.zeros((B, POOL_BUF, D), x_prompt.dtype)
    x1p = _pool_prompt(x_prompt, past_p, 0, pw, ps, g00, b00)
    ext_s = jnp.concatenate([state_pool[0], x_sample], axis=1)
    x1s = _pool_sample(ext_s.transpose(1, 0, 2), PAST_LEN, pw, ps, g00, b00)
    x1 = jnp.concatenate([x1p.reshape(n_p, D), x1s.reshape(Ts * Bs, D)], axis=0)
    pool_p = x_prompt[:, T - POOL_BUF:][None]
    pool_s = ext_s[:, Ts:][None]

    x2 = _peer_layer(x1, bf(peer_w_q[0]), peer_keys[0], bf(peer_u[0]), bf(peer_v[0].T),
                     row(ln_g[0, 1]), row(ln_b[0, 1]))

    x2p = x2[:n_p].reshape(B, T, D)
    x2s = x2[n_p:].reshape(Ts, Bs, D)
    xp_p = jnp.concatenate([jnp.zeros((B, 1, D), _f32), x2p[:, :-1]], axis=1)
    xp_s = jnp.concatenate([state_rwkv_shift[0][None], x2s[:-1]], axis=0)
    xprev = jnp.concatenate([xp_p.reshape(n_p, D), xp_s.reshape(Ts * Bs, D)], axis=0)
    shift_p = x2p[:, -1][None]
    shift_s = x2s[-1][None]

    g1 = rwkv_g1[0]
    g2 = rwkv_g2[0]
    gpad = (-g1.shape[1]) % V7X_LANES
    g1 = jnp.pad(g1, ((0, 0), (0, gpad)))
    g2 = jnp.pad(g2, ((0, gpad), (0, 0)))
    xr, xk, xv, hw, ha, hg = _rwkv_mix(x2, xprev, rwkv_mu[0], bf(rwkv_w1[0]), bf(rwkv_a1[0]), bf(g1))
    r, lw, k, v, a, g = _rwkv_proj(xr, xk, xv, bf(rwkv_w_rkv[0]), hw, ha, hg, bf(rwkv_w2[0]),
                                   bf(rwkv_a2[0]), bf(g2), row(rwkv_w0[0]), row(rwkv_a0[0]))

    def scan(time_major, s0_t):
        ops = [time_major(o) for o in (r, lw, k, v, a)]
        nt, nb = ops[0].shape[:2]
        ops = [_to_lanes(o, (nt, nb), hd) for o in ops]
        pars = [_head_param(p, nb, hd) for p in (rwkv_k_k[0], rwkv_k_a[0], rwkv_r_k[0].reshape(D),
                                                 rwkv_lnx_g[0], rwkv_lnx_b[0])]
        y, s_fin = _wkv(*ops, s0_t, *pars)
        y = _from_lanes(y, (nt, nb), hd)
        s_fin = s_fin.reshape(hd, hd, nb, H).transpose(2, 3, 1, 0)
        return y, s_fin

    y_p, wkv_p = scan(lambda o: o[:n_p].reshape(B, T, D).transpose(1, 0, 2),
                      jnp.zeros((hd, hd, B * H), _f32))
    s0_s = state_rwkv_wkv[0].transpose(3, 2, 0, 1).reshape(hd, hd, Bs * H)
    y_s, wkv_s = scan(lambda o: o[n_p:].reshape(Ts, Bs, D), s0_s)
    y = jnp.concatenate([y_p.transpose(1, 0, 2).reshape(n_p, D), y_s.reshape(Ts * Bs, D)], axis=0)
    x3 = _rwkv_out(y, g, bf(rwkv_w_o[0]), x2, row(ln_g[1, 0]), row(ln_b[1, 0]))

    x4 = _peer_layer(x3, bf(peer_w_q[1]), peer_keys[1], bf(peer_u[1]), bf(peer_v[1].T),
                     row(ln_g[1, 1]), row(ln_b[1, 1]))
    y_prompt = x4[:n_p].reshape(B, T, D)
    y_sample = x4[n_p:].reshape(Ts, Bs, D).transpose(1, 0, 2)
    return (y_prompt, y_sample, pool_p, pool_s, shift_p, shift_s, wkv_p[None], wkv_s[None])
```

```python
import functools
import math

import jax
import jax.numpy as jnp
from jax import lax
from jax.experimental import pallas as pl
from jax.experimental.pallas import tpu as pltpu

POOL_WINDOWS = (2, 4, 8, 16)
POOL_BUF = max(POOL_WINDOWS) - 1
POOL_HALO = POOL_BUF + 1
RWKV_HEAD = 64
GN_EPS = 64e-5
PEER_HEADS = 8
PEER_TOPK = 16
PEER_GATE_ROWS = 16
LN_EPS = 1e-5
DEPTH = 2
ALPHA = (2.0 * DEPTH) ** 0.25
PAST_LEN = 16384

V7X_LANES = 128
V7X_SUBLANES = 8
LN_ROWS = 64
WKV_PREP_STEPS = 8
WKV_KEY_UNROLL = 8
V7X_VMEM_LIMIT = 63 * 1024 * 1024
NEG_BIG = -3.0e38
LOG2E = 1.4426950408889634

_f32 = jnp.float32
_bf16 = jnp.bfloat16


def _tile(dim, pref, align=8):
    if dim <= pref:
        return dim
    for t in range(pref - pref % align, 0, -align):
        if dim % t == 0:
            return t
    raise ValueError((dim, pref, align))


def _params(sem):
    return pltpu.CompilerParams(dimension_semantics=sem, vmem_limit_bytes=V7X_VMEM_LIMIT)


def _layer_norm(z, g, b):
    mu = jnp.mean(z, axis=-1, keepdims=True)
    zc = z - mu
    var = jnp.mean(zc * zc, axis=-1, keepdims=True)
    return zc * lax.rsqrt(var + LN_EPS) * g + b


def _residual_layer_norm(o_ref, x_ref, g_ref, b_ref):
    n_rows = o_ref.shape[0]
    rows_per = LN_ROWS if n_rows % LN_ROWS == 0 else n_rows
    g = g_ref[...]
    b = b_ref[...]

    def body(i, carry):
        rows = pl.ds(pl.multiple_of(i * rows_per, rows_per), rows_per)
        z = o_ref[rows, :]
        if x_ref is not None:
            z = ALPHA * x_ref[rows, :] + z
        o_ref[rows, :] = _layer_norm(z, g, b)
        return carry

    lax.fori_loop(0, n_rows // rows_per, body, 0)


def _dot(a, b, precision=None):
    return jnp.dot(a, b, preferred_element_type=_f32, precision=precision)


def _dot_nt(a, b, precision=None):
    return lax.dot_general(a, b, (((1,), (1,)), ((), ())), preferred_element_type=_f32,
                           precision=precision)


def _dot_tn(a, b, precision=None):
    return lax.dot_general(a, b, (((0,), (0,)), ((), ())), preferred_element_type=_f32,
                           precision=precision)


def _pool_finish(x, window_sum, n_real, w_ref, scale_ref, g_ref, b_ref, z_ref):
    gc = x.shape[-1] // len(POOL_WINDOWS)
    for gi, w in enumerate(POOL_WINDOWS):
        cs = slice(gi * gc, (gi + 1) * gc)
        if isinstance(n_real, int):
            inv_cnt = 1.0 / float(min(n_real, w))
        else:
            inv_cnt = 1.0 / jnp.minimum(n_real, w).astype(_f32)
        d = window_sum(gi, w) * inv_cnt - x[:, cs]
        y = _dot(d.astype(_bf16), w_ref[gi]) * scale_ref[:, cs]
        z_ref[:, cs] = ALPHA * x[:, cs] + y
    _residual_layer_norm(z_ref, None, g_ref, b_ref)


def _pool_prompt_kernel(x_ref, halo_ref, w_ref, scale_ref, g_ref, b_ref, o_ref, ext_ref, *, start):
    tt = x_ref.shape[0]
    gc = x_ref.shape[1] // len(POOL_WINDOWS)
    ext_ref[0:POOL_HALO, :] = halo_ref[...]
    ext_ref[POOL_HALO:POOL_HALO + tt, :] = x_ref[...]
    t0 = pl.program_id(1) * tt
    n_real = lax.broadcasted_iota(jnp.int32, (tt, 1), 0) + (t0 + start + 1)

    def window_sum(gi, w):
        cs = slice(gi * gc, (gi + 1) * gc)
        s = ext_ref[POOL_HALO:POOL_HALO + tt, cs]
        for k in range(1, w):
            s = s + ext_ref[POOL_HALO - k:POOL_HALO - k + tt, cs]
        return s

    _pool_finish(x_ref[...], window_sum, n_real, w_ref, scale_ref, g_ref, b_ref, o_ref)


def _pool_prompt(x, past, start, w_bf, scale, g, b):
    B, T, D = x.shape
    tt = _tile(T, 256)
    nt = T // tt
    gc = D // len(POOL_WINDOWS)
    xe = jnp.concatenate([jnp.zeros((B, 1, D), x.dtype), past, x], axis=1)[:, :T]
    halo = xe.reshape(B, nt, tt, D)[:, :, :POOL_HALO]
    vec = pl.BlockSpec((1, D), lambda bi, ti: (0, 0))
    return pl.pallas_call(
        functools.partial(_pool_prompt_kernel, start=start),
        grid=(B, nt),
        in_specs=[pl.BlockSpec((None, tt, D), lambda bi, ti: (bi, ti, 0)),
                  pl.BlockSpec((None, None, POOL_HALO, D), lambda bi, ti: (bi, ti, 0, 0)),
                  pl.BlockSpec((len(POOL_WINDOWS), gc, gc), lambda bi, ti: (0, 0, 0)),
                  vec, vec, vec],
        out_specs=pl.BlockSpec((None, tt, D), lambda bi, ti: (bi, ti, 0)),
        out_shape=jax.ShapeDtypeStruct((B, T, D), _f32),
        scratch_shapes=[pltpu.VMEM((POOL_HALO + tt, D), _f32)],
        compiler_params=_params(("parallel", "arbitrary")),
        name="pool_prompt",
    )(x, halo, w_bf, scale, g, b)


def _pool_sample_kernel(ext_ref, w_ref, scale_ref, g_ref, b_ref, o_ref, z_ref, *, start):
    ts = o_ref.shape[0]
    gc = o_ref.shape[2] // len(POOL_WINDOWS)
    for t in range(ts):
        def window_sum(gi, w, t=t):
            cs = slice(gi * gc, (gi + 1) * gc)
            s = ext_ref[POOL_BUF + t, :, cs]
            for k in range(1, w):
                s = s + ext_ref[POOL_BUF + t - k, :, cs]
            return s

        _pool_finish(ext_ref[POOL_BUF + t], window_sum, start + t + 1, w_ref, scale_ref,
                     g_ref, b_ref, z_ref)
        o_ref[t] = z_ref[...]


def _pool_sample(ext_tm, start, w_bf, scale, g, b):
    te, Bs, D = ext_tm.shape
    ts = te - POOL_BUF
    bb = _tile(Bs, 16)
    gc = D // len(POOL_WINDOWS)
    vec = pl.BlockSpec((1, D), lambda bi: (0, 0))
    return pl.pallas_call(
        functools.partial(_pool_sample_kernel, start=start),
        grid=(Bs // bb,),
        in_specs=[pl.BlockSpec((te, bb, D), lambda bi: (0, bi, 0)),
                  pl.BlockSpec((len(POOL_WINDOWS), gc, gc), lambda bi: (0, 0, 0)),
                  vec, vec, vec],
        out_specs=pl.BlockSpec((ts, bb, D), lambda bi: (0, bi, 0)),
        out_shape=jax.ShapeDtypeStruct((ts, Bs, D), _f32),
        scratch_shapes=[pltpu.VMEM((bb, D), _f32)],
        compiler_params=_params(("parallel",)),
        name="pool_sample",
    )(ext_tm, w_bf, scale, g, b)


def _top_rows(s, n):
    rows = []
    for r in range(n):
        m = jnp.max(s, axis=0, keepdims=True)
        rows.append(m)
        if r + 1 < n:
            s = jnp.where(s == m, NEG_BIG, s)
    return rows


def _peer_stats_kernel(x_ref, wq_ref, keys_ref, s2_ref, thr_ref, c_ref):
    half = keys_ref.shape[2]
    q = _dot(x_ref[...].astype(_bf16), wq_ref[...])
    s1 = _dot_nt(keys_ref[0], q[:, :half], precision=lax.Precision.HIGHEST)
    s2 = _dot_nt(keys_ref[1], q[:, half:], precision=lax.Precision.HIGHEST)
    n = PEER_TOPK + 1
    t1 = _top_rows(s1, n)
    t2 = _top_rows(s2, n)
    cands = [t1[a] + t2[b] for a in range(n) for b in range(n) if (a + 1) * (b + 1) <= n]
    cands += [jnp.full_like(cands[0], NEG_BIG)] * ((-len(cands)) % V7X_SUBLANES)
    vals = _top_rows(jnp.concatenate(cands, axis=0), n)
    c0 = vals[0]
    z = jnp.zeros_like(c0)
    for r in range(PEER_TOPK):
        z = z + jnp.exp(vals[r] - c0)
    tau = 0.5 * (vals[PEER_TOPK - 1] + vals[PEER_TOPK])
    s2_ref[...] = s2 * LOG2E
    thr_ref[...] = (tau - s1) * LOG2E
    c_ref[...] = (s1 - (c0 + jnp.log(z))) * LOG2E


def _peer_stats(x, wq_bf, keys):
    N, D = x.shape
    n_keys, half = keys.shape[1], keys.shape[2]
    tn = _tile(N, 512)
    out = jax.ShapeDtypeStruct((PEER_HEADS, n_keys, N), _f32)
    ospec = pl.BlockSpec((None, n_keys, tn), lambda i, h: (h, 0, i))
    return pl.pallas_call(
        _peer_stats_kernel,
        grid=(N // tn, PEER_HEADS),
        in_specs=[pl.BlockSpec((tn, D), lambda i, h: (i, 0)),
                  pl.BlockSpec((D, 2 * half), lambda i, h: (0, h)),
                  pl.BlockSpec((2, n_keys, half), lambda i, h: (0, 0, 0))],
        out_specs=[ospec, ospec, ospec],
        out_shape=[out, out, out],
        compiler_params=_params(("parallel", "arbitrary")),
        name="peer_stats",
    )(x, wq_bf, keys)


def _gelu(h):
    return 0.5 * h * (1.0 + lax.erf(h * (1.0 / math.sqrt(2.0))))


def _peer_dense_kernel(x_hbm, u_ref, vt_ref, s2_ref, thr_ref, c_ref, g_ref, b_ref, o_ref,
                       xt_ref, acc_ref, ht_ref, at_ref, bc_ref, sem):
    i = pl.program_id(0)
    j = pl.program_id(1)
    n_keys = s2_ref.shape[1]
    te = u_ref.shape[0]
    tn, d = o_ref.shape
    sub = te // n_keys
    drow = d // sub
    slot = j % 2

    @pl.when(j == 0)
    def _():
        cp = pltpu.make_async_copy(x_hbm.at[pl.ds(pl.multiple_of(i * tn, tn), tn), :], o_ref, sem)
        cp.start()
        cp.wait()
        for cb in range(0, d, V7X_LANES):
            xt_ref[cb:cb + V7X_LANES, :] = o_ref[:, cb:cb + V7X_LANES].T.astype(_bf16)
        acc_ref[...] = jnp.zeros_like(acc_ref)
        at_ref[1] = jnp.zeros((te, tn), _bf16)

    ht_ref[...] = _dot(u_ref[...], xt_ref[...])
    a_prev = at_ref[1 - slot]
    for ii in range(sub):
        drows = slice(ii * drow, (ii + 1) * drow)
        acc_ref[drows, :] += _dot(vt_ref[drows, :], a_prev)
        i1 = jnp.minimum(j * sub + ii, n_keys - 1)
        for h in range(PEER_HEADS):
            bc_ref[ii, h] = jnp.broadcast_to(thr_ref[h, pl.ds(i1, 1), :], (V7X_SUBLANES, tn))
            bc_ref[ii, PEER_HEADS + h] = jnp.broadcast_to(c_ref[h, pl.ds(i1, 1), :],
                                                          (V7X_SUBLANES, tn))
        for r0 in range(0, n_keys, PEER_GATE_ROWS):
            gates = [None] * (PEER_GATE_ROWS // V7X_SUBLANES)
            for h in range(PEER_HEADS):
                thr = bc_ref[ii, h]
                c = bc_ref[ii, PEER_HEADS + h]
                for q in range(len(gates)):
                    s2 = s2_ref[h, r0 + q * V7X_SUBLANES:r0 + (q + 1) * V7X_SUBLANES, :]
                    term = jnp.where(s2 > thr, jnp.exp2(s2 + c), 0.0)
                    gates[q] = term if gates[q] is None else gates[q] + term
            gate = jnp.concatenate(gates, axis=0)
            rows = slice(ii * n_keys + r0, ii * n_keys + r0 + PEER_GATE_ROWS)
            at_ref[slot, rows, :] = (_gelu(ht_ref[rows, :]) * gate).astype(_bf16)

    @pl.when(j == pl.num_programs(1) - 1)
    def _():
        for cb in range(0, d, V7X_LANES):
            cs = slice(cb, cb + V7X_LANES)
            o_ref[:, cs] = ALPHA * o_ref[:, cs] + acc_ref[cs, :].T
        _residual_layer_norm(o_ref, None, g_ref, b_ref)


def _peer_dense(x, u_bf, vt_bf, s2t, thrt, ct, g, b):
    N, D = x.shape
    E = u_bf.shape[0]
    n_keys = s2t.shape[1]
    tn = _tile(N, 512)
    te = _tile(E, 4 * n_keys)
    nj = E // te
    vec = pl.BlockSpec((1, D), lambda i, j: (0, 0))
    once = pl.Buffered(1)
    stat = pl.BlockSpec((PEER_HEADS, n_keys, tn), lambda i, j: (0, 0, i), pipeline_mode=once)
    return pl.pallas_call(
        _peer_dense_kernel,
        grid=(N // tn, nj + 1),
        in_specs=[pl.BlockSpec(memory_space=pl.ANY),
                  pl.BlockSpec((te, D), lambda i, j: (jnp.minimum(j, nj - 1), 0)),
                  pl.BlockSpec((D, te), lambda i, j: (0, jnp.maximum(j - 1, 0))),
                  stat, stat, stat, vec, vec],
        out_specs=pl.BlockSpec((tn, D), lambda i, j: (i, 0)),
        out_shape=jax.ShapeDtypeStruct((N, D), _f32),
        scratch_shapes=[pltpu.VMEM((D, tn), _bf16), pltpu.VMEM((D, tn), _f32),
                        pltpu.VMEM((te, tn), _f32), pltpu.VMEM((2, te, tn), _bf16),
                        pltpu.VMEM((te // n_keys, 2 * PEER_HEADS, V7X_SUBLANES, tn), _f32),
                        pltpu.SemaphoreType.DMA(())],
        compiler_params=_params(("parallel", "arbitrary")),
        name="peer_dense",
    )(x, u_bf, vt_bf, s2t, thrt, ct, g, b)


def _peer_layer(x, wq_bf, keys, u_bf, vt_bf, g, b):
    s2t, thrt, ct = _peer_stats(x, wq_bf, keys)
    return _peer_dense(x, u_bf, vt_bf, s2t, thrt, ct, g, b)


def _rwkv_mix_body(x, xp, mu_ref, w1_ref, a1_ref, g1_ref,
                   xr_ref, xk_ref, xv_ref, hw_ref, ha_ref, hg_ref):
    xx = xp - x

    def mix(i):
        return (x + xx * mu_ref[i:i + 1, :]).astype(_bf16)

    xr_ref[...] = mix(0)
    xk_ref[...] = mix(2)
    xv_ref[...] = mix(3)
    hw_ref[...] = jnp.tanh(_dot(mix(1), w1_ref[...])).astype(_bf16)
    ha_ref[...] = _dot(mix(4), a1_ref[...]).astype(_bf16)
    hg_ref[...] = (1.0 / (1.0 + jnp.exp(-_dot(mix(5), g1_ref[...])))).astype(_bf16)


def _rwkv_mix_kernel(x_ref, xp_ref, *rest):
    _rwkv_mix_body(x_ref[...], xp_ref[...], *rest)


def _rwkv_mix_shift_kernel(x_ref, halo_ref, *rest):
    *refs, ext_ref = rest
    tm = x_ref.shape[0]
    ext_ref[V7X_SUBLANES - 1:V7X_SUBLANES, :] = halo_ref[...]
    ext_ref[V7X_SUBLANES:V7X_SUBLANES + tm, :] = x_ref[...]
    _rwkv_mix_body(x_ref[...], ext_ref[V7X_SUBLANES - 1:V7X_SUBLANES - 1 + tm, :], *refs)


def _rwkv_mix(x, mu, w1_bf, a1_bf, g1_bf, *, xp=None, seq_len=None):
    N, D = x.shape
    tm = _tile(N if seq_len is None else seq_len, 256)
    row = pl.BlockSpec((tm, D), lambda i: (i, 0))

    def full(a):
        return pl.BlockSpec(a.shape, lambda i: (0, 0))

    def out(cols, dt):
        return pl.BlockSpec((tm, cols), lambda i: (i, 0)), jax.ShapeDtypeStruct((N, cols), dt)

    outs = [out(D, _bf16), out(D, _bf16), out(D, _bf16), out(w1_bf.shape[1], _bf16),
            out(a1_bf.shape[1], _bf16), out(g1_bf.shape[1], _bf16)]
    weights = [mu, w1_bf, a1_bf, g1_bf]
    if xp is not None:
        body, second, second_spec, scratch = _rwkv_mix_kernel, xp, row, []
    else:
        nt = N // tm
        last = x.reshape(nt, tm, D)[:, tm - 1]
        prev_last = jnp.concatenate([jnp.zeros((1, D), x.dtype), last[:-1]], axis=0)
        starts = (jnp.arange(nt) % (seq_len // tm)) == 0
        second = jnp.where(starts[:, None], 0.0, prev_last).reshape(nt, 1, D)
        second_spec = pl.BlockSpec((None, 1, D), lambda i: (i, 0, 0))
        body = _rwkv_mix_shift_kernel
        scratch = [pltpu.VMEM((V7X_SUBLANES + tm, D), _f32)]
    return pl.pallas_call(
        body,
        grid=(N // tm,),
        in_specs=[row, second_spec] + [full(w) for w in weights],
        out_specs=[o[0] for o in outs],
        out_shape=[o[1] for o in outs],
        scratch_shapes=scratch,
        compiler_params=_params(("parallel",)),
        name="rwkv_mix",
    )(x, second, *weights)


def _rwkv_proj_kernel(xr_ref, xk_ref, xv_ref, wr_ref, wk_ref, wv_ref, hw_ref, ha_ref, hg_ref,
                      w2_ref, a2_ref, g2_ref, w0_ref, a0_ref,
                      r_ref, lw_ref, k_ref, v_ref, a_ref, g_ref):
    r_ref[...] = _dot(xr_ref[...], wr_ref[...])
    k_ref[...] = _dot(xk_ref[...], wk_ref[...])
    v_ref[...] = _dot(xv_ref[...], wv_ref[...])
    u = -(w0_ref[...] + _dot(hw_ref[...], w2_ref[...]))
    softplus = jnp.maximum(u, 0.0) + jnp.log(1.0 + jnp.exp(-jnp.abs(u)))
    lw_ref[...] = -jnp.exp(-softplus - 0.5)
    a_ref[...] = 1.0 / (1.0 + jnp.exp(-(a0_ref[...] + _dot(ha_ref[...], a2_ref[...]))))
    g_ref[...] = _dot(hg_ref[...], g2_ref[...])


def _rwkv_proj(xr, xk, xv, wrkv_bf, hw, ha, hg, w2_bf, a2_bf, g2_bf, w0, a0):
    N, D = xr.shape
    tm = _tile(N, 512)
    tc = _tile(D, 256)
    row = pl.BlockSpec((tm, D), lambda i, j: (i, 0))

    def wmat(m):
        return pl.BlockSpec((None, D, tc), lambda i, j, m=m: (m, 0, j))

    def rowfull(a):
        return pl.BlockSpec((tm, a.shape[1]), lambda i, j: (i, 0))

    def colblk(a):
        return pl.BlockSpec((a.shape[0], tc), lambda i, j: (0, j))

    ospec = pl.BlockSpec((tm, tc), lambda i, j: (i, j))
    oshape = jax.ShapeDtypeStruct((N, D), _f32)
    return pl.pallas_call(
        _rwkv_proj_kernel,
        grid=(N // tm, D // tc),
        in_specs=[row, row, row, wmat(0), wmat(1), wmat(2), rowfull(hw), rowfull(ha), rowfull(hg),
                  colblk(w2_bf), colblk(a2_bf), colblk(g2_bf), colblk(w0), colblk(a0)],
        out_specs=[ospec] * 6,
        out_shape=[oshape] * 6,
        compiler_params=_params(("parallel", "arbitrary")),
        name="rwkv_proj",
    )(xr, xk, xv, wrkv_bf, wrkv_bf, wrkv_bf, hw, ha, hg, w2_bf, a2_bf, g2_bf, w0, a0)


def _wkv_kernel(r_ref, lw_ref, k_ref, v_ref, a_ref, s0_ref, kk_ref, ka_ref, rk_ref, gg_ref, gb_ref,
                y_ref, s_ref, an_ref, dr_ref, dec_ref, bn_ref, km_ref, sc_ref):
    tc = r_ref.shape[0]
    hd = r_ref.shape[1]
    ps = WKV_PREP_STEPS if tc % WKV_PREP_STEPS == 0 else tc

    @pl.when(pl.program_id(1) == 0)
    def _():
        s_ref[...] = s0_ref[...]

    def prep(i, carry):
        ts = pl.ds(pl.multiple_of(i * ps, ps), ps)
        r = r_ref[ts]
        k = k_ref[ts]
        a = a_ref[ts]
        dec = jnp.exp(lw_ref[ts])
        kk = k * kk_ref[...]
        kk = kk * lax.rsqrt(jnp.maximum(jnp.sum(kk * kk, axis=1, keepdims=True), 1e-24))
        kmod = k * (1.0 + (a - 1.0) * ka_ref[...])
        bn = kk * a
        an_ref[ts] = -kk
        dr_ref[ts] = dec * r
        dec_ref[ts] = dec
        bn_ref[ts] = bn
        km_ref[ts] = kmod
        sc_ref[0, ts, :] = jnp.sum(bn * r, axis=1)
        sc_ref[1, ts, :] = jnp.sum(kmod * r, axis=1)
        sc_ref[2, ts, :] = jnp.sum(r * kmod * rk_ref[...], axis=1)
        return carry

    lax.fori_loop(0, tc // ps, prep, 0)

    def step(t, carry):
        v = v_ref[t]
        row = pl.ds(t, 1)
        ju = WKV_KEY_UNROLL if hd % WKV_KEY_UNROLL == 0 else hd

        def reduce_keys(jc, acc):
            sa, z = acc
            for jj in range(ju):
                j = jc * ju + jj
                sj = s_ref[j]
                sa = sa + sj * an_ref[t, pl.ds(j, 1), :]
                z = z + sj * dr_ref[t, pl.ds(j, 1), :]
            return sa, z

        sa, z = lax.fori_loop(0, hd // ju, reduce_keys, (jnp.zeros_like(v), jnp.zeros_like(v)))
        y_ref[t] = z + sa * sc_ref[0, row, :] + v * sc_ref[1, row, :]

        def update_keys(jc, c2):
            for jj in range(ju):
                j = jc * ju + jj
                s_ref[j] = (s_ref[j] * dec_ref[t, pl.ds(j, 1), :] + sa * bn_ref[t, pl.ds(j, 1), :]
                            + v * km_ref[t, pl.ds(j, 1), :])
            return c2

        lax.fori_loop(0, hd // ju, update_keys, 0)
        return carry

    lax.fori_loop(0, tc, step, 0)

    def finish(i, carry):
        ts = pl.ds(pl.multiple_of(i * ps, ps), ps)
        y = y_ref[ts]
        m = jnp.mean(y, axis=1, keepdims=True)
        yc = y - m
        var = jnp.mean(yc * yc, axis=1, keepdims=True)
        yn = yc * lax.rsqrt(var + GN_EPS) * gg_ref[...] + gb_ref[...]
        y_ref[ts] = yn + sc_ref[2, ts, :][:, None, :] * v_ref[ts]
        return carry

    lax.fori_loop(0, tc // ps, finish, 0)


def _wkv(r, lw, k, v, a, s0, kk_t, ka_t, rk_t, gg_t, gb_t):
    T, hd, L = r.shape
    tl = _tile(L, V7X_LANES)
    tc = _tile(T, 32)
    op = pl.BlockSpec((tc, hd, tl), lambda g, c: (c, 0, g))
    st = pl.BlockSpec((hd, hd, tl), lambda g, c: (0, 0, g))
    par = pl.BlockSpec((hd, tl), lambda g, c: (0, g))
    return pl.pallas_call(
        _wkv_kernel,
        grid=(L // tl, T // tc),
        in_specs=[op, op, op, op, op, st, par, par, par, par, par],
        out_specs=[op, st],
        out_shape=[jax.ShapeDtypeStruct((T, hd, L), _f32), jax.ShapeDtypeStruct((hd, hd, L), _f32)],
        scratch_shapes=[pltpu.VMEM((tc, hd, tl), _f32)] * 5 + [pltpu.VMEM((3, tc, tl), _f32)],
        compiler_params=_params(("parallel", "arbitrary")),
        name="wkv_scan",
    )(r, lw, k, v, a, s0, kk_t, ka_t, rk_t, gg_t, gb_t)


def _wkv_seq_kernel(r_ref, lw_ref, k_ref, v_ref, a_ref, kk_ref, ka_ref, rk_ref, gg_ref, gb_ref,
                    y_ref, s_ref, an_ref, dr_ref, dec_ref, bn_ref, km_ref, vt_ref, yt_ref, sc_ref):
    nb, tc, d = r_ref.shape
    hd = kk_ref.shape[0]
    nh = d // hd
    ps = WKV_PREP_STEPS if tc % WKV_PREP_STEPS == 0 else tc
    ju = WKV_KEY_UNROLL if hd % WKV_KEY_UNROLL == 0 else hd

    @pl.when(pl.program_id(1) == 0)
    def _():
        s_ref[...] = jnp.zeros_like(s_ref)

    def slab(ref, j):
        return jnp.concatenate([ref[b, :, j * nh:(j + 1) * nh] for b in range(nb)], axis=-1)

    zero = jnp.zeros((tc, nb * nh), _f32)
    ss, q, kr, bonus = zero, zero, zero, zero
    for j in range(hd):
        r = slab(r_ref, j)
        k = slab(k_ref, j)
        a = slab(a_ref, j)
        dec = jnp.exp(slab(lw_ref, j))
        vt_ref[:, j, :] = slab(v_ref, j)
        kk = k * kk_ref[j:j + 1, :]
        kmod = k * (1.0 + (a - 1.0) * ka_ref[j:j + 1, :])
        kka = kk * a
        an_ref[j] = kk
        bn_ref[j] = kka
        dr_ref[j] = dec * r
        dec_ref[j] = dec
        km_ref[j] = kmod
        ss = ss + kk * kk
        q = q + kka * r
        kr = kr + kmod * r
        bonus = bonus + r * kmod * rk_ref[j:j + 1, :]
    scale = lax.rsqrt(jnp.maximum(ss, 1e-24))
    sc_ref[0] = -scale
    sc_ref[1] = scale
    sc_ref[2] = q * scale
    sc_ref[3] = kr
    sc_ref[4] = bonus

    def step(t, carry):
        row = pl.ds(t, 1)
        v = vt_ref[t]

        def reduce_keys(jc, acc):
            sa, z = acc
            for jj in range(ju):
                j = jc * ju + jj
                sj = s_ref[j]
                sa = sa + sj * an_ref[j, row, :]
                z = z + sj * dr_ref[j, row, :]
            return sa, z

        sa, z = lax.fori_loop(0, hd // ju, reduce_keys, (jnp.zeros_like(v), jnp.zeros_like(v)))
        sa = sa * sc_ref[0, row, :]
        yt_ref[t] = z + sa * sc_ref[2, row, :] + v * sc_ref[3, row, :]
        sb = sa * sc_ref[1, row, :]

        def update_keys(jc, c2):
            for jj in range(ju):
                j = jc * ju + jj
                s_ref[j] = (s_ref[j] * dec_ref[j, row, :] + sb * bn_ref[j, row, :]
                            + v * km_ref[j, row, :])
            return c2

        lax.fori_loop(0, hd // ju, update_keys, 0)
        return carry

    lax.fori_loop(0, tc, step, 0)

    def finish(i, carry):
        ts = pl.ds(pl.multiple_of(i * ps, ps), ps)
        y = yt_ref[ts]
        m = jnp.mean(y, axis=1, keepdims=True)
        yc = y - m
        var = jnp.mean(yc * yc, axis=1, keepdims=True)
        yn = yc * lax.rsqrt(var + GN_EPS) * gg_ref[...] + gb_ref[...]
        yt_ref[ts] = yn + sc_ref[4, ts, :][:, None, :] * vt_ref[ts]
        return carry

    lax.fori_loop(0, tc // ps, finish, 0)
    for i in range(hd):
        yi = yt_ref[:, i, :]
        for b in range(nb):
            y_ref[b, :, i * nh:(i + 1) * nh] = yi[:, b * nh:(b + 1) * nh]


def _wkv_seq(r, lw, k, v, a, kk_t, ka_t, rk_t, gg_t, gb_t):
    B, T, D = r.shape
    hd, L = kk_t.shape
    nb = L // (D // hd)
    tc = _tile(T, 32)
    op = pl.BlockSpec((nb, tc, D), lambda g, c: (g, c, 0))
    st = pl.BlockSpec((hd, hd, L), lambda g, c: (0, 0, g))
    par = pl.BlockSpec((hd, L), lambda g, c: (0, 0))
    slabs = pltpu.VMEM((hd, tc, L), _f32)
    tiles = pltpu.VMEM((tc, hd, L), _f32)
    return pl.pallas_call(
        _wkv_seq_kernel,
        grid=(B // nb, T // tc),
        in_specs=[op, op, op, op, op, par, par, par, par, par],
        out_specs=[op, st],
        out_shape=[jax.ShapeDtypeStruct((B, T, D), _f32),
                   jax.ShapeDtypeStruct((hd, hd, (B // nb) * L), _f32)],
        scratch_shapes=[slabs] * 5 + [tiles, tiles, pltpu.VMEM((5, tc, L), _f32)],
        compiler_params=_params(("parallel", "arbitrary")),
        name="wkv_seq",
    )(r, lw, k, v, a, kk_t, ka_t, rk_t, gg_t, gb_t)


def _rwkv_out_kernel(y_ref, g_ref, wo_ref, x_ref, lg_ref, lb_ref, o_ref):
    kt = pl.program_id(1)

    @pl.when(kt == 0)
    def _():
        o_ref[...] = jnp.zeros_like(o_ref)

    o_ref[...] += _dot((y_ref[...] * g_ref[...]).astype(_bf16), wo_ref[...])

    @pl.when(kt == pl.num_programs(1) - 1)
    def _():
        _residual_layer_norm(o_ref, x_ref, lg_ref, lb_ref)


def _rwkv_out(y, g, wo_bf, x, lg, lb):
    N, D = x.shape
    tm = _tile(N, 512)
    tk = _tile(D, 512)
    vec = pl.BlockSpec((1, D), lambda i, kt: (0, 0))
    return pl.pallas_call(
        _rwkv_out_kernel,
        grid=(N // tm, D // tk),
        in_specs=[pl.BlockSpec((tm, tk), lambda i, kt: (i, kt)),
                  pl.BlockSpec((tm, tk), lambda i, kt: (i, kt)),
                  pl.BlockSpec((tk, D), lambda i, kt: (kt, 0)),
                  pl.BlockSpec((tm, D), lambda i, kt: (i, 0)),
                  vec, vec],
        out_specs=pl.BlockSpec((tm, D), lambda i, kt: (i, 0)),
        out_shape=jax.ShapeDtypeStruct((N, D), _f32),
        compiler_params=_params(("parallel", "arbitrary")),
        name="rwkv_out",
    )(y, g, wo_bf, x, lg, lb)


def _to_lanes(a, steps, batch, hd):
    H = a.shape[-1] // hd
    return a.reshape(steps, batch, hd, H).transpose(0, 2, 1, 3).reshape(steps, hd, batch * H)


def _from_lanes(a, steps, batch, hd):
    H = a.shape[-1] // batch
    return a.reshape(steps, hd, batch, H).transpose(0, 2, 1, 3).reshape(steps * batch, hd * H)


def _head_param(p, batch, hd):
    H = p.shape[-1] // hd
    return jnp.tile(p.reshape(H, hd).T[:, None, :], (1, batch, 1)).reshape(hd, batch * H)


def _perm_last(w, hd):
    H = w.shape[-1] // hd
    return w.reshape(w.shape[:-1] + (H, hd)).swapaxes(-1, -2).reshape(w.shape)


def _perm_first(w, hd):
    H = w.shape[0] // hd
    return w.reshape((H, hd) + w.shape[1:]).swapaxes(0, 1).reshape(w.shape)


def kernel(x_prompt, x_sample, state_pool, state_rwkv_shift, state_rwkv_wkv, pool_w, pool_scale,
           rwkv_mu, rwkv_w_rkv, rwkv_w0, rwkv_w1, rwkv_w2, rwkv_a0, rwkv_a1, rwkv_a2, rwkv_g1,
           rwkv_g2, rwkv_k_k, rwkv_k_a, rwkv_r_k, rwkv_lnx_g, rwkv_lnx_b, rwkv_w_o, peer_w_q,
           peer_keys, peer_u, peer_v, ln_g, ln_b):
    B, T, D = x_prompt.shape
    Bs, Ts, _ = x_sample.shape
    hd = RWKV_HEAD
    H = D // hd
    n_p = B * T
    bf = lambda a: a.astype(_bf16)
    row = lambda a: a.reshape(1, D)

    pw = bf(pool_w[0])
    ps = row(pool_scale[0])
    g00, b00 = row(ln_g[0, 0]), row(ln_b[0, 0])
    past_p = jnp.zeros((B, POOL_BUF, D), x_prompt.dtype)
    x1p = _pool_prompt(x_prompt, past_p, 0, pw, ps, g00, b00).reshape(n_p, D)
    ext_s = jnp.concatenate([state_pool[0], x_sample], axis=1)
    x1s = _pool_sample(ext_s.transpose(1, 0, 2), PAST_LEN, pw, ps, g00, b00).reshape(Ts * Bs, D)
    pool_p = x_prompt[:, T - POOL_BUF:][None]
    pool_s = ext_s[:, Ts:][None]

    peer0 = (bf(peer_w_q[0]), peer_keys[0], bf(peer_u[0]), bf(peer_v[0].T),
             row(ln_g[0, 1]), row(ln_b[0, 1]))
    x2p = _peer_layer(x1p, *peer0)
    x2s = _peer_layer(x1s, *peer0)
    shift_p = x2p.reshape(B, T, D)[:, -1][None]
    shift_s = x2s.reshape(Ts, Bs, D)[-1][None]

    g1 = rwkv_g1[0]
    g2 = rwkv_g2[0]
    gpad = (-g1.shape[1]) % V7X_LANES
    g1 = jnp.pad(g1, ((0, 0), (0, gpad)))
    g2 = jnp.pad(g2, ((0, gpad), (0, 0)))
    mixw = (rwkv_mu[0], bf(rwkv_w1[0]), bf(rwkv_a1[0]), bf(g1))
    pc = lambda w: _perm_last(w, hd)
    projw = (bf(pc(rwkv_w_rkv[0])), bf(pc(rwkv_w2[0])), bf(pc(rwkv_a2[0])), bf(pc(g2)),
             row(pc(rwkv_w0[0])), row(pc(rwkv_a0[0])))
    wo_bf = bf(_perm_first(rwkv_w_o[0], hd))
    head_pars = (rwkv_k_k[0], rwkv_k_a[0], rwkv_r_k[0].reshape(D), rwkv_lnx_g[0], rwkv_lnx_b[0])
    lg10, lb10 = row(ln_g[1, 0]), row(ln_b[1, 0])

    def project(x, **shift):
        xr, xk, xv, hw, ha, hg = _rwkv_mix(x, *mixw, **shift)
        return _rwkv_proj(xr, xk, xv, projw[0], hw, ha, hg, *projw[1:])

    r, lw, k, v, a, g = project(x2p, seq_len=T)
    nb = 2 if B % 2 == 0 else 1
    seq = lambda o: o.reshape(B, T, D)
    y_p, s_p = _wkv_seq(seq(r), seq(lw), seq(k), seq(v), seq(a),
                        *[_head_param(p, nb, hd) for p in head_pars])
    wkv_p = s_p.reshape(hd, hd, B, H).transpose(2, 3, 1, 0)
    x3p = _rwkv_out(y_p.reshape(n_p, D), g, wo_bf, x2p, lg10, lb10)

    xprev_s = jnp.concatenate([state_rwkv_shift[0], x2s[:(Ts - 1) * Bs]], axis=0)
    r, lw, k, v, a, g = project(x2s, xp=xprev_s)
    s0_s = state_rwkv_wkv[0].transpose(3, 2, 0, 1).reshape(hd, hd, Bs * H)
    y_s, s_s = _wkv(*[_to_lanes(o, Ts, Bs, hd) for o in (r, lw, k, v, a)], s0_s,
                    *[_head_param(p, Bs, hd) for p in head_pars])
    wkv_s = s_s.reshape(hd, hd, Bs, H).transpose(2, 3, 1, 0)
    x3s = _rwkv_out(_from_lanes(y_s, Ts, Bs, hd), g, wo_bf, x2s, lg10, lb10)

    peer1 = (bf(peer_w_q[1]), peer_keys[1], bf(peer_u[1]), bf(peer_v[1].T),
             row(ln_g[1, 1]), row(ln_b[1, 1]))
    y_prompt = _peer_layer(x3p, *peer1).reshape(B, T, D)
    y_sample = _peer_layer(x3s, *peer1).reshape(Ts, Bs, D).transpose(1, 0, 2)
    return (y_prompt, y_sample, pool_p, pool_s, shift_p, shift_s, wkv_p[None], wkv_s[None])
```

```python
import functools
import math

import jax
import jax.numpy as jnp
from jax import lax
from jax.experimental import pallas as pl
from jax.experimental.pallas import tpu as pltpu

POOL_WINDOWS = (2, 4, 8, 16)
POOL_BUF = max(POOL_WINDOWS) - 1
POOL_HALO = POOL_BUF + 1
RWKV_HEAD = 64
GN_EPS = 64e-5
PEER_HEADS = 8
PEER_TOPK = 16
PEER_GATE_ROWS = 16
LN_EPS = 1e-5
DEPTH = 2
ALPHA = (2.0 * DEPTH) ** 0.25
PAST_LEN = 16384

V7X_LANES = 128
V7X_SUBLANES = 8
LN_ROWS = 64
WKV_PREP_STEPS = 8
WKV_KEY_UNROLL = 8
V7X_VMEM_LIMIT = 63 * 1024 * 1024
NEG_BIG = -3.0e38
LOG2E = 1.4426950408889634

_f32 = jnp.float32
_bf16 = jnp.bfloat16


def _tile(dim, pref, align=8):
    if dim <= pref:
        return dim
    for t in range(pref - pref % align, 0, -align):
        if dim % t == 0:
            return t
    raise ValueError((dim, pref, align))


def _params(sem):
    return pltpu.CompilerParams(dimension_semantics=sem, vmem_limit_bytes=V7X_VMEM_LIMIT)


def _layer_norm(z, g, b):
    mu = jnp.mean(z, axis=-1, keepdims=True)
    zc = z - mu
    var = jnp.mean(zc * zc, axis=-1, keepdims=True)
    return zc * lax.rsqrt(var + LN_EPS) * g + b


def _residual_layer_norm(o_ref, x_ref, g_ref, b_ref):
    n_rows = o_ref.shape[0]
    rows_per = LN_ROWS if n_rows % LN_ROWS == 0 else n_rows
    g = g_ref[...]
    b = b_ref[...]

    def body(i, carry):
        rows = pl.ds(pl.multiple_of(i * rows_per, rows_per), rows_per)
        z = o_ref[rows, :]
        if x_ref is not None:
            z = ALPHA * x_ref[rows, :] + z
        o_ref[rows, :] = _layer_norm(z, g, b)
        return carry

    lax.fori_loop(0, n_rows // rows_per, body, 0)


def _dot(a, b, precision=None):
    return jnp.dot(a, b, preferred_element_type=_f32, precision=precision)


def _dot_nt(a, b, precision=None):
    return lax.dot_general(a, b, (((1,), (1,)), ((), ())), preferred_element_type=_f32,
                           precision=precision)


def _dot_tn(a, b, precision=None):
    return lax.dot_general(a, b, (((0,), (0,)), ((), ())), preferred_element_type=_f32,
                           precision=precision)


def _pool_finish(x, window_sum, n_real, w_ref, scale_ref, g_ref, b_ref, z_ref):
    gc = x.shape[-1] // len(POOL_WINDOWS)
    for gi, w in enumerate(POOL_WINDOWS):
        cs = slice(gi * gc, (gi + 1) * gc)
        if isinstance(n_real, int):
            inv_cnt = 1.0 / float(min(n_real, w))
        else:
            inv_cnt = 1.0 / jnp.minimum(n_real, w).astype(_f32)
        d = window_sum(gi, w) * inv_cnt - x[:, cs]
        dh = d.astype(_bf16)
        dl = (d - dh.astype(_f32)).astype(_bf16)
        wh, wl = w_ref[gi], w_ref[len(POOL_WINDOWS) + gi]
        y = (_dot(dh, wh) + (_dot(dl, wh) + _dot(dh, wl))) * scale_ref[:, cs]
        z_ref[:, cs] = ALPHA * x[:, cs] + y
    _residual_layer_norm(z_ref, None, g_ref, b_ref)


def _pool_prompt_kernel(x_ref, halo_ref, w_ref, scale_ref, g_ref, b_ref, o_ref, ext_ref, *, start):
    tt = x_ref.shape[0]
    gc = x_ref.shape[1] // len(POOL_WINDOWS)
    ext_ref[0:POOL_HALO, :] = halo_ref[...]
    ext_ref[POOL_HALO:POOL_HALO + tt, :] = x_ref[...]
    t0 = pl.program_id(1) * tt
    n_real = lax.broadcasted_iota(jnp.int32, (tt, 1), 0) + (t0 + start + 1)

    def window_sum(gi, w):
        cs = slice(gi * gc, (gi + 1) * gc)
        s = ext_ref[POOL_HALO:POOL_HALO + tt, cs]
        for k in range(1, w):
            s = s + ext_ref[POOL_HALO - k:POOL_HALO - k + tt, cs]
        return s

    _pool_finish(x_ref[...], window_sum, n_real, w_ref, scale_ref, g_ref, b_ref, o_ref)


def _pool_prompt(x, past, start, w_bf, scale, g, b):
    B, T, D = x.shape
    tt = _tile(T, 256)
    nt = T // tt
    gc = D // len(POOL_WINDOWS)
    halo = jnp.concatenate([jnp.zeros((B, 1, D), x.dtype), past], axis=1)[:, None]
    if nt > 1:
        halo = jnp.concatenate([halo, x.reshape(B, nt, tt, D)[:, :-1, tt - POOL_HALO:]], axis=1)
    vec = pl.BlockSpec((1, D), lambda bi, ti: (0, 0))
    return pl.pallas_call(
        functools.partial(_pool_prompt_kernel, start=start),
        grid=(B, nt),
        in_specs=[pl.BlockSpec((None, tt, D), lambda bi, ti: (bi, ti, 0)),
                  pl.BlockSpec((None, None, POOL_HALO, D), lambda bi, ti: (bi, ti, 0, 0)),
                  pl.BlockSpec(w_bf.shape, lambda bi, ti: (0, 0, 0), pipeline_mode=pl.Buffered(1)),
                  vec, vec, vec],
        out_specs=pl.BlockSpec((None, tt, D), lambda bi, ti: (bi, ti, 0)),
        out_shape=jax.ShapeDtypeStruct((B, T, D), _f32),
        scratch_shapes=[pltpu.VMEM((POOL_HALO + tt, D), _f32)],
        compiler_params=_params(("parallel", "arbitrary")),
        name="pool_prompt",
    )(x, halo, w_bf, scale, g, b)


def _pool_sample_kernel(ext_ref, w_ref, scale_ref, g_ref, b_ref, o_ref, z_ref, *, start):
    ts = o_ref.shape[0]
    gc = o_ref.shape[2] // len(POOL_WINDOWS)
    for t in range(ts):
        def window_sum(gi, w, t=t):
            cs = slice(gi * gc, (gi + 1) * gc)
            s = ext_ref[POOL_BUF + t, :, cs]
            for k in range(1, w):
                s = s + ext_ref[POOL_BUF + t - k, :, cs]
            return s

        _pool_finish(ext_ref[POOL_BUF + t], window_sum, start + t + 1, w_ref, scale_ref,
                     g_ref, b_ref, z_ref)
        o_ref[t] = z_ref[...]


def _pool_sample(ext_tm, start, w_bf, scale, g, b):
    te, Bs, D = ext_tm.shape
    ts = te - POOL_BUF
    bb = _tile(Bs, 16)
    gc = D // len(POOL_WINDOWS)
    vec = pl.BlockSpec((1, D), lambda bi: (0, 0))
    return pl.pallas_call(
        functools.partial(_pool_sample_kernel, start=start),
        grid=(Bs // bb,),
        in_specs=[pl.BlockSpec((te, bb, D), lambda bi: (0, bi, 0)),
                  pl.BlockSpec(w_bf.shape, lambda bi: (0, 0, 0)),
                  vec, vec, vec],
        out_specs=pl.BlockSpec((ts, bb, D), lambda bi: (0, bi, 0)),
        out_shape=jax.ShapeDtypeStruct((ts, Bs, D), _f32),
        scratch_shapes=[pltpu.VMEM((bb, D), _f32)],
        compiler_params=_params(("parallel",)),
        name="pool_sample",
    )(ext_tm, w_bf, scale, g, b)


def _top_rows(s, n):
    rows = []
    for r in range(n):
        m = jnp.max(s, axis=0, keepdims=True)
        rows.append(m)
        if r + 1 < n:
            s = jnp.where(s == m, NEG_BIG, s)
    return rows


def _peer_stats_kernel(x_ref, wq_ref, wql_ref, keys_ref, s2_ref, thr_ref, c_ref, xh_ref, xl_ref,
                       *, split):
    half = keys_ref.shape[2]

    @pl.when(pl.program_id(1) == 0)
    def _():
        x = x_ref[...]
        xh = x.astype(_bf16)
        xh_ref[...] = xh
        if split:
            xl_ref[...] = (x - xh.astype(_f32)).astype(_bf16)

    q = _dot(xh_ref[...], wq_ref[...])
    if split:
        q = q + (_dot(xl_ref[...], wq_ref[...]) + _dot(xh_ref[...], wql_ref[...]))
    s1 = _dot_nt(keys_ref[0], q[:, :half], precision=lax.Precision.HIGHEST)
    s2 = _dot_nt(keys_ref[1], q[:, half:], precision=lax.Precision.HIGHEST)
    n = PEER_TOPK + 1
    t1 = _top_rows(s1, n)
    t2 = _top_rows(s2, n)
    cands = [t1[a] + t2[b] for a in range(n) for b in range(n) if (a + 1) * (b + 1) <= n]
    cands += [jnp.full_like(cands[0], NEG_BIG)] * ((-len(cands)) % V7X_SUBLANES)
    vals = _top_rows(jnp.concatenate(cands, axis=0), n)
    c0 = vals[0]
    z = jnp.zeros_like(c0)
    for r in range(PEER_TOPK):
        z = z + jnp.exp(vals[r] - c0)
    tau = 0.5 * (vals[PEER_TOPK - 1] + vals[PEER_TOPK])
    s2_ref[...] = s2 * LOG2E
    thr_ref[...] = (tau - s1) * LOG2E
    c_ref[...] = (s1 - (c0 + jnp.log(z))) * LOG2E


def _peer_stats(x, wq, keys):
    N, D = x.shape
    n_keys, half = keys.shape[1], keys.shape[2]
    tn = _tile(N, 512)
    split = wq.dtype == _f32
    wq_hi = wq.astype(_bf16)
    wq_lo = (wq - wq_hi.astype(_f32)).astype(_bf16) if split else wq_hi
    out = jax.ShapeDtypeStruct((PEER_HEADS, n_keys, N), _f32)
    ospec = pl.BlockSpec((None, n_keys, tn), lambda i, h: (h, 0, i))
    wspec = pl.BlockSpec((D, 2 * half), lambda i, h: (0, h))
    half_tile = pltpu.VMEM((tn, D), _bf16)
    return pl.pallas_call(
        functools.partial(_peer_stats_kernel, split=split),
        grid=(N // tn, PEER_HEADS),
        in_specs=[pl.BlockSpec((tn, D), lambda i, h: (i, 0)), wspec, wspec,
                  pl.BlockSpec((2, n_keys, half), lambda i, h: (0, 0, 0))],
        out_specs=[ospec, ospec, ospec],
        out_shape=[out, out, out],
        scratch_shapes=[half_tile, half_tile],
        compiler_params=_params(("parallel", "arbitrary")),
        name="peer_stats",
    )(x, wq_hi, wq_lo, keys)


def _gelu(h):
    return 0.5 * h * (1.0 + lax.erf(h * (1.0 / math.sqrt(2.0))))


def _peer_dense_kernel(x_hbm, u_ref, vt_ref, s2_ref, thr_ref, c_ref, g_ref, b_ref, o_ref,
                       xt_ref, acc_ref, ht_ref, at_ref, bc_ref, sem):
    i = pl.program_id(0)
    j = pl.program_id(1)
    n_keys = s2_ref.shape[1]
    te = u_ref.shape[0]
    tn, d = o_ref.shape
    sub = te // n_keys
    drow = d // sub
    slot = j % 2

    @pl.when(j == 0)
    def _():
        cp = pltpu.make_async_copy(x_hbm.at[pl.ds(pl.multiple_of(i * tn, tn), tn), :], o_ref, sem)
        cp.start()
        cp.wait()
        for cb in range(0, d, V7X_LANES):
            xt_ref[cb:cb + V7X_LANES, :] = o_ref[:, cb:cb + V7X_LANES].T.astype(_bf16)
        acc_ref[...] = jnp.zeros_like(acc_ref)
        at_ref[1] = jnp.zeros((te, tn), _bf16)

    ht_ref[...] = _dot(u_ref[...], xt_ref[...])
    a_prev = at_ref[1 - slot]
    def key_block(ii, carry):
        drows = pl.ds(pl.multiple_of(ii * drow, drow), drow)
        acc_ref[drows, :] += _dot(vt_ref[drows, :], a_prev)
        i1 = jnp.minimum(j * sub + ii, n_keys - 1)
        for h in range(PEER_HEADS):
            bc_ref[h] = jnp.broadcast_to(thr_ref[h, pl.ds(i1, 1), :], (V7X_SUBLANES, tn))
            bc_ref[PEER_HEADS + h] = jnp.broadcast_to(c_ref[h, pl.ds(i1, 1), :],
                                                      (V7X_SUBLANES, tn))
        for r0 in range(0, n_keys, PEER_GATE_ROWS):
            gates = [None] * (PEER_GATE_ROWS // V7X_SUBLANES)
            for h in range(PEER_HEADS):
                thr = bc_ref[h]
                c = bc_ref[PEER_HEADS + h]
                for q in range(len(gates)):
                    s2 = s2_ref[h, r0 + q * V7X_SUBLANES:r0 + (q + 1) * V7X_SUBLANES, :]
                    term = jnp.where(s2 > thr, jnp.exp2(s2 + c), 0.0)
                    gates[q] = term if gates[q] is None else gates[q] + term
            gate = jnp.concatenate(gates, axis=0)
            rows = pl.ds(pl.multiple_of(ii * n_keys + r0, PEER_GATE_ROWS), PEER_GATE_ROWS)
            at_ref[slot, rows, :] = (_gelu(ht_ref[rows, :]) * gate).astype(_bf16)
        return carry

    lax.fori_loop(0, sub, key_block, 0)

    @pl.when(j == pl.num_programs(1) - 1)
    def _():
        for cb in range(0, d, V7X_LANES):
            cs = slice(cb, cb + V7X_LANES)
            o_ref[:, cs] = ALPHA * o_ref[:, cs] + acc_ref[cs, :].T
        _residual_layer_norm(o_ref, None, g_ref, b_ref)


def _peer_dense(x, u_bf, vt_bf, s2t, thrt, ct, g, b):
    N, D = x.shape
    E = u_bf.shape[0]
    n_keys = s2t.shape[1]
    tn = _tile(N, 512)
    te = _tile(E, 4 * n_keys)
    nj = E // te
    vec = pl.BlockSpec((1, D), lambda i, j: (0, 0))
    once = pl.Buffered(1)
    stat = pl.BlockSpec((PEER_HEADS, n_keys, tn), lambda i, j: (0, 0, i), pipeline_mode=once)
    return pl.pallas_call(
        _peer_dense_kernel,
        grid=(N // tn, nj + 1),
        in_specs=[pl.BlockSpec(memory_space=pl.ANY),
                  pl.BlockSpec((te, D), lambda i, j: (jnp.minimum(j, nj - 1), 0)),
                  pl.BlockSpec((D, te), lambda i, j: (0, jnp.maximum(j - 1, 0))),
                  stat, stat, stat, vec, vec],
        out_specs=pl.BlockSpec((tn, D), lambda i, j: (i, 0)),
        out_shape=jax.ShapeDtypeStruct((N, D), _f32),
        scratch_shapes=[pltpu.VMEM((D, tn), _bf16), pltpu.VMEM((D, tn), _f32),
                        pltpu.VMEM((te, tn), _f32), pltpu.VMEM((2, te, tn), _bf16),
                        pltpu.VMEM((2 * PEER_HEADS, V7X_SUBLANES, tn), _f32),
                        pltpu.SemaphoreType.DMA(())],
        compiler_params=_params(("parallel", "arbitrary")),
        name="peer_dense",
    )(x, u_bf, vt_bf, s2t, thrt, ct, g, b)


def _expert_tables_kernel(u_ref, v_ref, ub_ref, vt_ref):
    ub_ref[...] = u_ref[...].astype(_bf16)
    for cb in range(0, v_ref.shape[1], V7X_LANES):
        vt_ref[cb:cb + V7X_LANES, :] = v_ref[:, cb:cb + V7X_LANES].T.astype(_bf16)


def _expert_tables(peer_u, peer_v, layer):
    _, E, D = peer_u.shape
    te = _tile(E, 512, V7X_LANES)
    src = pl.BlockSpec((None, te, D), lambda j: (layer, j, 0))
    return pl.pallas_call(
        _expert_tables_kernel,
        grid=(E // te,),
        in_specs=[src, src],
        out_specs=[pl.BlockSpec((te, D), lambda j: (j, 0)), pl.BlockSpec((D, te), lambda j: (0, j))],
        out_shape=[jax.ShapeDtypeStruct((E, D), _bf16), jax.ShapeDtypeStruct((D, E), _bf16)],
        compiler_params=_params(("parallel",)),
        name="expert_tables",
    )(peer_u, peer_v)


def _peer_layer(x, wq_bf, keys, u_bf, vt_bf, g, b):
    s2t, thrt, ct = _peer_stats(x, wq_bf, keys)
    return _peer_dense(x, u_bf, vt_bf, s2t, thrt, ct, g, b)


def _rwkv_mix_body(x, xp, mu_ref, w1_ref, a1_ref, g1_ref,
                   xr_ref, xk_ref, xv_ref, hw_ref, ha_ref, hg_ref):
    xx = xp - x

    def mix(i):
        return (x + xx * mu_ref[i:i + 1, :]).astype(_bf16)

    xr_ref[...] = mix(0)
    xk_ref[...] = mix(2)
    xv_ref[...] = mix(3)
    hw_ref[...] = jnp.tanh(_dot(mix(1), w1_ref[...])).astype(_bf16)
    ha_ref[...] = _dot(mix(4), a1_ref[...]).astype(_bf16)
    hg_ref[...] = (1.0 / (1.0 + jnp.exp(-_dot(mix(5), g1_ref[...])))).astype(_bf16)


def _rwkv_mix_kernel(x_ref, xp_ref, *rest):
    _rwkv_mix_body(x_ref[...], xp_ref[...], *rest)


def _rwkv_mix_shift_kernel(x_ref, halo_ref, *rest):
    *refs, ext_ref = rest
    tm = x_ref.shape[0]
    ext_ref[V7X_SUBLANES - 1:V7X_SUBLANES, :] = halo_ref[...]
    ext_ref[V7X_SUBLANES:V7X_SUBLANES + tm, :] = x_ref[...]
    _rwkv_mix_body(x_ref[...], ext_ref[V7X_SUBLANES - 1:V7X_SUBLANES - 1 + tm, :], *refs)


def _rwkv_mix(x, mu, w1_bf, a1_bf, g1_bf, *, xp=None, seq_len=None):
    N, D = x.shape
    tm = _tile(N if seq_len is None else seq_len, 256)
    row = pl.BlockSpec((tm, D), lambda i: (i, 0))

    def full(a):
        return pl.BlockSpec(a.shape, lambda i: (0, 0))

    def out(cols, dt):
        return pl.BlockSpec((tm, cols), lambda i: (i, 0)), jax.ShapeDtypeStruct((N, cols), dt)

    outs = [out(D, _bf16), out(D, _bf16), out(D, _bf16), out(w1_bf.shape[1], _bf16),
            out(a1_bf.shape[1], _bf16), out(g1_bf.shape[1], _bf16)]
    weights = [mu, w1_bf, a1_bf, g1_bf]
    if xp is not None:
        body, second, second_spec, scratch = _rwkv_mix_kernel, xp, row, []
    else:
        nt = N // tm
        last = x.reshape(nt, tm, D)[:, tm - 1]
        prev_last = jnp.concatenate([jnp.zeros((1, D), x.dtype), last[:-1]], axis=0)
        starts = (jnp.arange(nt) % (seq_len // tm)) == 0
        second = jnp.where(starts[:, None], 0.0, prev_last).reshape(nt, 1, D)
        second_spec = pl.BlockSpec((None, 1, D), lambda i: (i, 0, 0))
        body = _rwkv_mix_shift_kernel
        scratch = [pltpu.VMEM((V7X_SUBLANES + tm, D), _f32)]
    return pl.pallas_call(
        body,
        grid=(N // tm,),
        in_specs=[row, second_spec] + [full(w) for w in weights],
        out_specs=[o[0] for o in outs],
        out_shape=[o[1] for o in outs],
        scratch_shapes=scratch,
        compiler_params=_params(("parallel",)),
        name="rwkv_mix",
    )(x, second, *weights)


def _rwkv_proj_kernel(xr_ref, xk_ref, xv_ref, wr_ref, wk_ref, wv_ref, hw_ref, ha_ref, hg_ref,
                      w2_ref, a2_ref, g2_ref, w0_ref, a0_ref,
                      r_ref, lw_ref, k_ref, v_ref, a_ref, g_ref):
    r_ref[...] = _dot(xr_ref[...], wr_ref[...])
    k_ref[...] = _dot(xk_ref[...], wk_ref[...])
    v_ref[...] = _dot(xv_ref[...], wv_ref[...])
    u = -(w0_ref[...] + _dot(hw_ref[...], w2_ref[...]))
    softplus = jnp.maximum(u, 0.0) + jnp.log(1.0 + jnp.exp(-jnp.abs(u)))
    lw_ref[...] = -jnp.exp(-softplus - 0.5)
    a_ref[...] = 1.0 / (1.0 + jnp.exp(-(a0_ref[...] + _dot(ha_ref[...], a2_ref[...]))))
    g_ref[...] = _dot(hg_ref[...], g2_ref[...])


def _rwkv_proj(xr, xk, xv, wrkv_bf, hw, ha, hg, w2_bf, a2_bf, g2_bf, w0, a0):
    N, D = xr.shape
    tm = _tile(N, 512)
    tc = _tile(D, 256)
    row = pl.BlockSpec((tm, D), lambda i, j: (i, 0))

    def wmat(m):
        return pl.BlockSpec((None, D, tc), lambda i, j, m=m: (m, 0, j))

    def rowfull(a):
        return pl.BlockSpec((tm, a.shape[1]), lambda i, j: (i, 0))

    def colblk(a):
        return pl.BlockSpec((a.shape[0], tc), lambda i, j: (0, j))

    ospec = pl.BlockSpec((tm, tc), lambda i, j: (i, j))
    oshape = jax.ShapeDtypeStruct((N, D), _f32)
    return pl.pallas_call(
        _rwkv_proj_kernel,
        grid=(N // tm, D // tc),
        in_specs=[row, row, row, wmat(0), wmat(1), wmat(2), rowfull(hw), rowfull(ha), rowfull(hg),
                  colblk(w2_bf), colblk(a2_bf), colblk(g2_bf), colblk(w0), colblk(a0)],
        out_specs=[ospec] * 6,
        out_shape=[oshape] * 6,
        compiler_params=_params(("parallel", "arbitrary")),
        name="rwkv_proj",
    )(xr, xk, xv, wrkv_bf, wrkv_bf, wrkv_bf, hw, ha, hg, w2_bf, a2_bf, g2_bf, w0, a0)


def _wkv_kernel(r_ref, lw_ref, k_ref, v_ref, a_ref, s0_ref, kk_ref, ka_ref, rk_ref, gg_ref, gb_ref,
                y_ref, s_ref, an_ref, dr_ref, dec_ref, bn_ref, km_ref, sc_ref):
    tc = r_ref.shape[0]
    hd = r_ref.shape[1]
    ps = WKV_PREP_STEPS if tc % WKV_PREP_STEPS == 0 else tc

    @pl.when(pl.program_id(1) == 0)
    def _():
        s_ref[...] = s0_ref[...]

    def prep(i, carry):
        ts = pl.ds(pl.multiple_of(i * ps, ps), ps)
        r = r_ref[ts]
        k = k_ref[ts]
        a = a_ref[ts]
        dec = jnp.exp(lw_ref[ts])
        kk = k * kk_ref[...]
        kk = kk * lax.rsqrt(jnp.maximum(jnp.sum(kk * kk, axis=1, keepdims=True), 1e-24))
        kmod = k * (1.0 + (a - 1.0) * ka_ref[...])
        bn = kk * a
        an_ref[ts] = -kk
        dr_ref[ts] = dec * r
        dec_ref[ts] = dec
        bn_ref[ts] = bn
        km_ref[ts] = kmod
        sc_ref[0, ts, :] = jnp.sum(bn * r, axis=1)
        sc_ref[1, ts, :] = jnp.sum(kmod * r, axis=1)
        sc_ref[2, ts, :] = jnp.sum(r * kmod * rk_ref[...], axis=1)
        return carry

    lax.fori_loop(0, tc // ps, prep, 0)

    def step(t, carry):
        v = v_ref[t]
        row = pl.ds(t, 1)
        ju = WKV_KEY_UNROLL if hd % WKV_KEY_UNROLL == 0 else hd

        def reduce_keys(jc, acc):
            sa, z = acc
            for jj in range(ju):
                j = jc * ju + jj
                sj = s_ref[j]
                sa = sa + sj * an_ref[t, pl.ds(j, 1), :]
                z = z + sj * dr_ref[t, pl.ds(j, 1), :]
            return sa, z

        sa, z = lax.fori_loop(0, hd // ju, reduce_keys, (jnp.zeros_like(v), jnp.zeros_like(v)))
        y_ref[t] = z + sa * sc_ref[0, row, :] + v * sc_ref[1, row, :]

        def update_keys(jc, c2):
            for jj in range(ju):
                j = jc * ju + jj
                s_ref[j] = (s_ref[j] * dec_ref[t, pl.ds(j, 1), :] + sa * bn_ref[t, pl.ds(j, 1), :]
                            + v * km_ref[t, pl.ds(j, 1), :])
            return c2

        lax.fori_loop(0, hd // ju, update_keys, 0)
        return carry

    lax.fori_loop(0, tc, step, 0)

    def finish(i, carry):
        ts = pl.ds(pl.multiple_of(i * ps, ps), ps)
        y = y_ref[ts]
        m = jnp.mean(y, axis=1, keepdims=True)
        yc = y - m
        var = jnp.mean(yc * yc, axis=1, keepdims=True)
        yn = yc * lax.rsqrt(var + GN_EPS) * gg_ref[...] + gb_ref[...]
        y_ref[ts] = yn + sc_ref[2, ts, :][:, None, :] * v_ref[ts]
        return carry

    lax.fori_loop(0, tc // ps, finish, 0)


def _wkv(r, lw, k, v, a, s0, kk_t, ka_t, rk_t, gg_t, gb_t):
    T, hd, L = r.shape
    tl = _tile(L, V7X_LANES)
    tc = _tile(T, 32)
    op = pl.BlockSpec((tc, hd, tl), lambda g, c: (c, 0, g))
    st = pl.BlockSpec((hd, hd, tl), lambda g, c: (0, 0, g))
    par = pl.BlockSpec((hd, tl), lambda g, c: (0, g))
    return pl.pallas_call(
        _wkv_kernel,
        grid=(L // tl, T // tc),
        in_specs=[op, op, op, op, op, st, par, par, par, par, par],
        out_specs=[op, st],
        out_shape=[jax.ShapeDtypeStruct((T, hd, L), _f32), jax.ShapeDtypeStruct((hd, hd, L), _f32)],
        scratch_shapes=[pltpu.VMEM((tc, hd, tl), _f32)] * 5 + [pltpu.VMEM((3, tc, tl), _f32)],
        compiler_params=_params(("parallel", "arbitrary")),
        name="wkv_scan",
    )(r, lw, k, v, a, s0, kk_t, ka_t, rk_t, gg_t, gb_t)


def _wkv_seq_kernel(r_ref, lw_ref, k_ref, v_ref, a_ref, kk_ref, ka_ref, rk_ref, gg_ref, gb_ref,
                    y_ref, s_ref, an_ref, dr_ref, dec_ref, bn_ref, km_ref, vt_ref, yt_ref, sc_ref):
    nb, tc, d = r_ref.shape
    hd = kk_ref.shape[0]
    nh = d // hd
    ps = WKV_PREP_STEPS if tc % WKV_PREP_STEPS == 0 else tc
    ju = WKV_KEY_UNROLL if hd % WKV_KEY_UNROLL == 0 else hd

    @pl.when(pl.program_id(1) == 0)
    def _():
        s_ref[...] = jnp.zeros_like(s_ref)

    def slab(ref, j):
        return jnp.concatenate([ref[b, :, j * nh:(j + 1) * nh] for b in range(nb)], axis=-1)

    zero = jnp.zeros((tc, nb * nh), _f32)
    ss, q, kr, bonus = zero, zero, zero, zero
    for j in range(hd):
        r = slab(r_ref, j)
        k = slab(k_ref, j)
        a = slab(a_ref, j)
        dec = jnp.exp(slab(lw_ref, j))
        vt_ref[:, j, :] = slab(v_ref, j)
        kk = k * kk_ref[j:j + 1, :]
        kmod = k * (1.0 + (a - 1.0) * ka_ref[j:j + 1, :])
        kka = kk * a
        an_ref[j] = kk
        bn_ref[j] = kka
        dr_ref[j] = dec * r
        dec_ref[j] = dec
        km_ref[j] = kmod
        ss = ss + kk * kk
        q = q + kka * r
        kr = kr + kmod * r
        bonus = bonus + r * kmod * rk_ref[j:j + 1, :]
    scale = lax.rsqrt(jnp.maximum(ss, 1e-24))
    sc_ref[0] = -scale
    sc_ref[1] = scale
    sc_ref[2] = q * scale
    sc_ref[3] = kr
    sc_ref[4] = bonus

    def step(t, carry):
        row = pl.ds(t, 1)
        v = vt_ref[t]

        def reduce_keys(jc, acc):
            sa, z = acc
            for jj in range(ju):
                j = jc * ju + jj
                sj = s_ref[j]
                sa = sa + sj * an_ref[j, row, :]
                z = z + sj * dr_ref[j, row, :]
            return sa, z

        sa, z = lax.fori_loop(0, hd // ju, reduce_keys, (jnp.zeros_like(v), jnp.zeros_like(v)))
        sa = sa * sc_ref[0, row, :]
        yt_ref[t] = z + sa * sc_ref[2, row, :] + v * sc_ref[3, row, :]
        sb = sa * sc_ref[1, row, :]

        def update_keys(jc, c2):
            for jj in range(ju):
                j = jc * ju + jj
                s_ref[j] = (s_ref[j] * dec_ref[j, row, :] + sb * bn_ref[j, row, :]
                            + v * km_ref[j, row, :])
            return c2

        lax.fori_loop(0, hd // ju, update_keys, 0)
        return carry

    lax.fori_loop(0, tc, step, 0)

    def finish(i, carry):
        ts = pl.ds(pl.multiple_of(i * ps, ps), ps)
        y = yt_ref[ts]
        m = jnp.mean(y, axis=1, keepdims=True)
        yc = y - m
        var = jnp.mean(yc * yc, axis=1, keepdims=True)
        yn = yc * lax.rsqrt(var + GN_EPS) * gg_ref[...] + gb_ref[...]
        yt_ref[ts] = yn + sc_ref[4, ts, :][:, None, :] * vt_ref[ts]
        return carry

    lax.fori_loop(0, tc // ps, finish, 0)
    for i in range(hd):
        yi = yt_ref[:, i, :]
        for b in range(nb):
            y_ref[b, :, i * nh:(i + 1) * nh] = yi[:, b * nh:(b + 1) * nh]


def _wkv_seq(r, lw, k, v, a, kk_t, ka_t, rk_t, gg_t, gb_t):
    B, T, D = r.shape
    hd, L = kk_t.shape
    nb = L // (D // hd)
    tc = _tile(T, 32)
    op = pl.BlockSpec((nb, tc, D), lambda g, c: (g, c, 0))
    st = pl.BlockSpec((hd, hd, L), lambda g, c: (0, 0, g))
    par = pl.BlockSpec((hd, L), lambda g, c: (0, 0))
    slabs = pltpu.VMEM((hd, tc, L), _f32)
    tiles = pltpu.VMEM((tc, hd, L), _f32)
    return pl.pallas_call(
        _wkv_seq_kernel,
        grid=(B // nb, T // tc),
        in_specs=[op, op, op, op, op, par, par, par, par, par],
        out_specs=[op, st],
        out_shape=[jax.ShapeDtypeStruct((B, T, D), _f32),
                   jax.ShapeDtypeStruct((hd, hd, (B // nb) * L), _f32)],
        scratch_shapes=[slabs] * 5 + [tiles, tiles, pltpu.VMEM((5, tc, L), _f32)],
        compiler_params=_params(("parallel", "arbitrary")),
        name="wkv_seq",
    )(r, lw, k, v, a, kk_t, ka_t, rk_t, gg_t, gb_t)


def _rwkv_out_kernel(y_ref, g_ref, wo_ref, x_ref, lg_ref, lb_ref, o_ref):
    kt = pl.program_id(1)

    @pl.when(kt == 0)
    def _():
        o_ref[...] = jnp.zeros_like(o_ref)

    o_ref[...] += _dot((y_ref[...] * g_ref[...]).astype(_bf16), wo_ref[...])

    @pl.when(kt == pl.num_programs(1) - 1)
    def _():
        _residual_layer_norm(o_ref, x_ref, lg_ref, lb_ref)


def _rwkv_out(y, g, wo_bf, x, lg, lb):
    N, D = x.shape
    tm = _tile(N, 512)
    tk = _tile(D, 512)
    vec = pl.BlockSpec((1, D), lambda i, kt: (0, 0))
    return pl.pallas_call(
        _rwkv_out_kernel,
        grid=(N // tm, D // tk),
        in_specs=[pl.BlockSpec((tm, tk), lambda i, kt: (i, kt)),
                  pl.BlockSpec((tm, tk), lambda i, kt: (i, kt)),
                  pl.BlockSpec((tk, D), lambda i, kt: (kt, 0)),
                  pl.BlockSpec((tm, D), lambda i, kt: (i, 0)),
                  vec, vec],
        out_specs=pl.BlockSpec((tm, D), lambda i, kt: (i, 0)),
        out_shape=jax.ShapeDtypeStruct((N, D), _f32),
        compiler_params=_params(("parallel", "arbitrary")),
        name="rwkv_out",
    )(y, g, wo_bf, x, lg, lb)


def _to_lanes(a, steps, batch, hd):
    H = a.shape[-1] // hd
    return a.reshape(steps, batch, hd, H).transpose(0, 2, 1, 3).reshape(steps, hd, batch * H)


def _from_lanes(a, steps, batch, hd):
    H = a.shape[-1] // batch
    return a.reshape(steps, hd, batch, H).transpose(0, 2, 1, 3).reshape(steps * batch, hd * H)


def _head_param(p, batch, hd):
    H = p.shape[-1] // hd
    return jnp.tile(p.reshape(H, hd).T[:, None, :], (1, batch, 1)).reshape(hd, batch * H)


def _perm_last(w, hd):
    H = w.shape[-1] // hd
    return w.reshape(w.shape[:-1] + (H, hd)).swapaxes(-1, -2).reshape(w.shape)


def _perm_first(w, hd):
    H = w.shape[0] // hd
    return w.reshape((H, hd) + w.shape[1:]).swapaxes(0, 1).reshape(w.shape)


def kernel(x_prompt, x_sample, state_pool, state_rwkv_shift, state_rwkv_wkv, pool_w, pool_scale,
           rwkv_mu, rwkv_w_rkv, rwkv_w0, rwkv_w1, rwkv_w2, rwkv_a0, rwkv_a1, rwkv_a2, rwkv_g1,
           rwkv_g2, rwkv_k_k, rwkv_k_a, rwkv_r_k, rwkv_lnx_g, rwkv_lnx_b, rwkv_w_o, peer_w_q,
           peer_keys, peer_u, peer_v, ln_g, ln_b):
    B, T, D = x_prompt.shape
    Bs, Ts, _ = x_sample.shape
    hd = RWKV_HEAD
    H = D // hd
    n_p = B * T
    bf = lambda a: a.astype(_bf16)
    row = lambda a: a.reshape(1, D)

    pw_hi = bf(pool_w[0])
    pw = jnp.concatenate([pw_hi, bf(pool_w[0] - pw_hi.astype(_f32))], axis=0)
    ps = row(pool_scale[0])
    g00, b00 = row(ln_g[0, 0]), row(ln_b[0, 0])
    past_p = jnp.zeros((B, POOL_BUF, D), x_prompt.dtype)
    x1p = _pool_prompt(x_prompt, past_p, 0, pw, ps, g00, b00).reshape(n_p, D)
    ext_s = jnp.concatenate([state_pool[0], x_sample], axis=1)
    x1s = _pool_sample(ext_s.transpose(1, 0, 2), PAST_LEN, pw, ps, g00, b00).reshape(Ts * Bs, D)
    pool_p = x_prompt[:, T - POOL_BUF:][None]
    pool_s = ext_s[:, Ts:][None]

    peer0 = (peer_w_q[0], peer_keys[0], *_expert_tables(peer_u, peer_v, 0),
             row(ln_g[0, 1]), row(ln_b[0, 1]))
    x2p = _peer_layer(x1p, *peer0)
    x2s = _peer_layer(x1s, *peer0)
    shift_p = x2p.reshape(B, T, D)[:, -1][None]
    shift_s = x2s.reshape(Ts, Bs, D)[-1][None]

    g1 = rwkv_g1[0]
    g2 = rwkv_g2[0]
    gpad = (-g1.shape[1]) % V7X_LANES
    g1 = jnp.pad(g1, ((0, 0), (0, gpad)))
    g2 = jnp.pad(g2, ((0, gpad), (0, 0)))
    mixw = (rwkv_mu[0], bf(rwkv_w1[0]), bf(rwkv_a1[0]), bf(g1))
    pc = lambda w: _perm_last(w, hd)
    projw = (bf(pc(rwkv_w_rkv[0])), bf(pc(rwkv_w2[0])), bf(pc(rwkv_a2[0])), bf(pc(g2)),
             row(pc(rwkv_w0[0])), row(pc(rwkv_a0[0])))
    wo_bf = bf(_perm_first(rwkv_w_o[0], hd))
    head_pars = (rwkv_k_k[0], rwkv_k_a[0], rwkv_r_k[0].reshape(D), rwkv_lnx_g[0], rwkv_lnx_b[0])
    lg10, lb10 = row(ln_g[1, 0]), row(ln_b[1, 0])

    def project(x, **shift):
        xr, xk, xv, hw, ha, hg = _rwkv_mix(x, *mixw, **shift)
        return _rwkv_proj(xr, xk, xv, projw[0], hw, ha, hg, *projw[1:])

    r, lw, k, v, a, g = project(x2p, seq_len=T)
    nb = 2 if B % 2 == 0 else 1
    seq = lambda o: o.reshape(B, T, D)
    y_p, s_p = _wkv_seq(seq(r), seq(lw), seq(k), seq(v), seq(a),
                        *[_head_param(p, nb, hd) for p in head_pars])
    wkv_p = s_p.reshape(hd, hd, B, H).transpose(2, 3, 1, 0)
    x3p = _rwkv_out(y_p.reshape(n_p, D), g, wo_bf, x2p, lg10, lb10)

    xprev_s = jnp.concatenate([state_rwkv_shift[0], x2s[:(Ts - 1) * Bs]], axis=0)
    r, lw, k, v, a, g = project(x2s, xp=xprev_s)
    s0_s = state_rwkv_wkv[0].transpose(3, 2, 0, 1).reshape(hd, hd, Bs * H)
    y_s, s_s = _wkv(*[_to_lanes(o, Ts, Bs, hd) for o in (r, lw, k, v, a)], s0_s,
                    *[_head_param(p, Bs, hd) for p in head_pars])
    wkv_s = s_s.reshape(hd, hd, Bs, H).transpose(2, 3, 1, 0)
    x3s = _rwkv_out(_from_lanes(y_s, Ts, Bs, hd), g, wo_bf, x2s, lg10, lb10)

    peer1 = (bf(peer_w_q[1]), peer_keys[1], *_expert_tables(peer_u, peer_v, 1),
             row(ln_g[1, 1]), row(ln_b[1, 1]))
    y_prompt = _peer_layer(x3p, *peer1).reshape(B, T, D)
    y_sample = _peer_layer(x3s, *peer1).reshape(Ts, Bs, D).transpose(1, 0, 2)
    return (y_prompt, y_sample, pool_p, pool_s, shift_p, shift_s, wkv_p[None], wkv_s[None])
```

```python
import functools
import math

import jax
import jax.numpy as jnp
from jax import lax
from jax.experimental import pallas as pl
from jax.experimental.pallas import tpu as pltpu

POOL_WINDOWS = (2, 4, 8, 16)
POOL_BUF = max(POOL_WINDOWS) - 1
POOL_HALO = POOL_BUF + 1
RWKV_HEAD = 64
GN_EPS = 64e-5
PEER_HEADS = 8
PEER_TOPK = 16
PEER_GATE_ROWS = 16
LN_EPS = 1e-5
DEPTH = 2
ALPHA = (2.0 * DEPTH) ** 0.25
PAST_LEN = 16384

V7X_LANES = 128
V7X_SUBLANES = 8
LN_ROWS = 64
WKV_PREP_STEPS = 8
WKV_KEY_UNROLL = 8
V7X_VMEM_LIMIT = 63 * 1024 * 1024
NEG_BIG = -3.0e38
LOG2E = 1.4426950408889634

_f32 = jnp.float32
_bf16 = jnp.bfloat16


def _tile(dim, pref, align=8):
    if dim <= pref:
        return dim
    for t in range(pref - pref % align, 0, -align):
        if dim % t == 0:
            return t
    raise ValueError((dim, pref, align))


def _params(sem):
    return pltpu.CompilerParams(dimension_semantics=sem, vmem_limit_bytes=V7X_VMEM_LIMIT)


def _layer_norm(z, g, b):
    mu = jnp.mean(z, axis=-1, keepdims=True)
    zc = z - mu
    var = jnp.mean(zc * zc, axis=-1, keepdims=True)
    return zc * lax.rsqrt(var + LN_EPS) * g + b


def _residual_layer_norm(o_ref, x_ref, g_ref, b_ref):
    n_rows = o_ref.shape[0]
    rows_per = LN_ROWS if n_rows % LN_ROWS == 0 else n_rows
    g = g_ref[...]
    b = b_ref[...]

    def body(i, carry):
        rows = pl.ds(pl.multiple_of(i * rows_per, rows_per), rows_per)
        z = o_ref[rows, :]
        if x_ref is not None:
            z = ALPHA * x_ref[rows, :] + z
        o_ref[rows, :] = _layer_norm(z, g, b)
        return carry

    lax.fori_loop(0, n_rows // rows_per, body, 0)


def _dot(a, b, precision=None):
    return jnp.dot(a, b, preferred_element_type=_f32, precision=precision)


def _dot_nt(a, b, precision=None):
    return lax.dot_general(a, b, (((1,), (1,)), ((), ())), preferred_element_type=_f32,
                           precision=precision)


def _dot_tn(a, b, precision=None):
    return lax.dot_general(a, b, (((0,), (0,)), ((), ())), preferred_element_type=_f32,
                           precision=precision)


def _pool_finish(x, window_sum, n_real, w_ref, scale_ref, g_ref, b_ref, z_ref):
    gc = x.shape[-1] // len(POOL_WINDOWS)
    for gi, w in enumerate(POOL_WINDOWS):
        cs = slice(gi * gc, (gi + 1) * gc)
        if isinstance(n_real, int):
            inv_cnt = 1.0 / float(min(n_real, w))
        else:
            inv_cnt = 1.0 / jnp.minimum(n_real, w).astype(_f32)
        d = window_sum(gi, w) * inv_cnt - x[:, cs]
        dh = d.astype(_bf16)
        dl = (d - dh.astype(_f32)).astype(_bf16)
        wh, wl = w_ref[gi], w_ref[len(POOL_WINDOWS) + gi]
        y = (_dot(dh, wh) + (_dot(dl, wh) + _dot(dh, wl))) * scale_ref[:, cs]
        z_ref[:, cs] = ALPHA * x[:, cs] + y
    _residual_layer_norm(z_ref, None, g_ref, b_ref)


def _pool_prompt_kernel(x_ref, halo_ref, w_ref, scale_ref, g_ref, b_ref, o_ref, ext_ref, *, start):
    tt = x_ref.shape[0]
    gc = x_ref.shape[1] // len(POOL_WINDOWS)
    ext_ref[0:POOL_HALO, :] = halo_ref[...]
    ext_ref[POOL_HALO:POOL_HALO + tt, :] = x_ref[...]
    t0 = pl.program_id(1) * tt
    n_real = lax.broadcasted_iota(jnp.int32, (tt, 1), 0) + (t0 + start + 1)

    def window_sum(gi, w):
        cs = slice(gi * gc, (gi + 1) * gc)
        s = ext_ref[POOL_HALO:POOL_HALO + tt, cs]
        for k in range(1, w):
            s = s + ext_ref[POOL_HALO - k:POOL_HALO - k + tt, cs]
        return s

    _pool_finish(x_ref[...], window_sum, n_real, w_ref, scale_ref, g_ref, b_ref, o_ref)


def _pool_prompt(x, past, start, w_bf, scale, g, b):
    B, T, D = x.shape
    tt = _tile(T, 256)
    nt = T // tt
    gc = D // len(POOL_WINDOWS)
    halo = jnp.concatenate([jnp.zeros((B, 1, D), x.dtype), past], axis=1)[:, None]
    if nt > 1:
        halo = jnp.concatenate([halo, x.reshape(B, nt, tt, D)[:, :-1, tt - POOL_HALO:]], axis=1)
    vec = pl.BlockSpec((1, D), lambda bi, ti: (0, 0))
    return pl.pallas_call(
        functools.partial(_pool_prompt_kernel, start=start),
        grid=(B, nt),
        in_specs=[pl.BlockSpec((None, tt, D), lambda bi, ti: (bi, ti, 0)),
                  pl.BlockSpec((None, None, POOL_HALO, D), lambda bi, ti: (bi, ti, 0, 0)),
                  pl.BlockSpec(w_bf.shape, lambda bi, ti: (0, 0, 0), pipeline_mode=pl.Buffered(1)),
                  vec, vec, vec],
        out_specs=pl.BlockSpec((None, tt, D), lambda bi, ti: (bi, ti, 0)),
        out_shape=jax.ShapeDtypeStruct((B, T, D), _f32),
        scratch_shapes=[pltpu.VMEM((POOL_HALO + tt, D), _f32)],
        compiler_params=_params(("parallel", "arbitrary")),
        name="pool_prompt",
    )(x, halo, w_bf, scale, g, b)


def _pool_sample_kernel(ext_ref, w_ref, scale_ref, g_ref, b_ref, o_ref, z_ref, *, start):
    ts = o_ref.shape[0]
    gc = o_ref.shape[2] // len(POOL_WINDOWS)
    for t in range(ts):
        def window_sum(gi, w, t=t):
            cs = slice(gi * gc, (gi + 1) * gc)
            s = ext_ref[POOL_BUF + t, :, cs]
            for k in range(1, w):
                s = s + ext_ref[POOL_BUF + t - k, :, cs]
            return s

        _pool_finish(ext_ref[POOL_BUF + t], window_sum, start + t + 1, w_ref, scale_ref,
                     g_ref, b_ref, z_ref)
        o_ref[t] = z_ref[...]


def _pool_sample(ext_tm, start, w_bf, scale, g, b):
    te, Bs, D = ext_tm.shape
    ts = te - POOL_BUF
    bb = _tile(Bs, 16)
    gc = D // len(POOL_WINDOWS)
    vec = pl.BlockSpec((1, D), lambda bi: (0, 0))
    return pl.pallas_call(
        functools.partial(_pool_sample_kernel, start=start),
        grid=(Bs // bb,),
        in_specs=[pl.BlockSpec((te, bb, D), lambda bi: (0, bi, 0)),
                  pl.BlockSpec(w_bf.shape, lambda bi: (0, 0, 0)),
                  vec, vec, vec],
        out_specs=pl.BlockSpec((ts, bb, D), lambda bi: (0, bi, 0)),
        out_shape=jax.ShapeDtypeStruct((ts, Bs, D), _f32),
        scratch_shapes=[pltpu.VMEM((bb, D), _f32)],
        compiler_params=_params(("parallel",)),
        name="pool_sample",
    )(ext_tm, w_bf, scale, g, b)


def _top_rows(s, n):
    rows = []
    for r in range(n):
        m = jnp.max(s, axis=0, keepdims=True)
        rows.append(m)
        if r + 1 < n:
            s = jnp.where(s == m, NEG_BIG, s)
    return rows


def _peer_stats_kernel(x_ref, wq_ref, wql_ref, keys_ref, s2_ref, thr_ref, c_ref, xh_ref, xl_ref,
                       *, split):
    half = keys_ref.shape[2]

    @pl.when(pl.program_id(1) == 0)
    def _():
        x = x_ref[...]
        xh = x.astype(_bf16)
        xh_ref[...] = xh
        if split:
            xl_ref[...] = (x - xh.astype(_f32)).astype(_bf16)

    q = _dot(xh_ref[...], wq_ref[...])
    if split:
        q = q + (_dot(xl_ref[...], wq_ref[...]) + _dot(xh_ref[...], wql_ref[...]))
    s1 = _dot_nt(keys_ref[0], q[:, :half], precision=lax.Precision.HIGHEST)
    s2 = _dot_nt(keys_ref[1], q[:, half:], precision=lax.Precision.HIGHEST)
    n = PEER_TOPK + 1
    t1 = _top_rows(s1, n)
    t2 = _top_rows(s2, n)
    cands = [t1[a] + t2[b] for a in range(n) for b in range(n) if (a + 1) * (b + 1) <= n]
    cands += [jnp.full_like(cands[0], NEG_BIG)] * ((-len(cands)) % V7X_SUBLANES)
    vals = _top_rows(jnp.concatenate(cands, axis=0), n)
    c0 = vals[0]
    z = jnp.zeros_like(c0)
    for r in range(PEER_TOPK):
        z = z + jnp.exp(vals[r] - c0)
    tau = 0.5 * (vals[PEER_TOPK - 1] + vals[PEER_TOPK])
    s2_ref[...] = s2 * LOG2E
    thr_ref[...] = (tau - s1) * LOG2E
    c_ref[...] = (s1 - (c0 + jnp.log(z))) * LOG2E


def _peer_stats(x, wq, keys):
    N, D = x.shape
    n_keys, half = keys.shape[1], keys.shape[2]
    tn = _tile(N, 512)
    split = wq.dtype == _f32
    wq_hi = wq.astype(_bf16)
    wq_lo = (wq - wq_hi.astype(_f32)).astype(_bf16) if split else wq_hi
    out = jax.ShapeDtypeStruct((PEER_HEADS, n_keys, N), _f32)
    ospec = pl.BlockSpec((None, n_keys, tn), lambda i, h: (h, 0, i))
    wspec = pl.BlockSpec((D, 2 * half), lambda i, h: (0, h))
    half_tile = pltpu.VMEM((tn, D), _bf16)
    return pl.pallas_call(
        functools.partial(_peer_stats_kernel, split=split),
        grid=(N // tn, PEER_HEADS),
        in_specs=[pl.BlockSpec((tn, D), lambda i, h: (i, 0)), wspec, wspec,
                  pl.BlockSpec((2, n_keys, half), lambda i, h: (0, 0, 0))],
        out_specs=[ospec, ospec, ospec],
        out_shape=[out, out, out],
        scratch_shapes=[half_tile, half_tile],
        compiler_params=_params(("parallel", "arbitrary")),
        name="peer_stats",
    )(x, wq_hi, wq_lo, keys)


def _gelu(h):
    return 0.5 * h * (1.0 + lax.erf(h * (1.0 / math.sqrt(2.0))))


def _peer_dense_kernel(x_hbm, u_ref, vt_ref, s2_ref, thr_ref, c_ref, g_ref, b_ref, o_ref,
                       xt_ref, acc_ref, ht_ref, at_ref, bc_ref, sem):
    i = pl.program_id(0)
    j = pl.program_id(1)
    n_keys = s2_ref.shape[1]
    te = u_ref.shape[0]
    tn, d = o_ref.shape
    sub = te // n_keys
    drow = d // sub
    slot = j % 2

    @pl.when(j == 0)
    def _():
        cp = pltpu.make_async_copy(x_hbm.at[pl.ds(pl.multiple_of(i * tn, tn), tn), :], o_ref, sem)
        cp.start()
        cp.wait()
        for cb in range(0, d, V7X_LANES):
            xt_ref[cb:cb + V7X_LANES, :] = o_ref[:, cb:cb + V7X_LANES].T.astype(_bf16)
        acc_ref[...] = jnp.zeros_like(acc_ref)
        at_ref[1] = jnp.zeros((te, tn), _bf16)

    ht_ref[...] = _dot(u_ref[...], xt_ref[...])
    a_prev = at_ref[1 - slot]
    for ii in range(sub):
        drows = slice(ii * drow, (ii + 1) * drow)
        acc_ref[drows, :] += _dot(vt_ref[drows, :], a_prev)
        i1 = jnp.minimum(j * sub + ii, n_keys - 1)
        for h in range(PEER_HEADS):
            bc_ref[ii, h] = jnp.broadcast_to(thr_ref[h, pl.ds(i1, 1), :], (V7X_SUBLANES, tn))
            bc_ref[ii, PEER_HEADS + h] = jnp.broadcast_to(c_ref[h, pl.ds(i1, 1), :],
                                                          (V7X_SUBLANES, tn))
        for r0 in range(0, n_keys, PEER_GATE_ROWS):
            gates = [None] * (PEER_GATE_ROWS // V7X_SUBLANES)
            for h in range(PEER_HEADS):
                thr = bc_ref[ii, h]
                c = bc_ref[ii, PEER_HEADS + h]
                for q in range(len(gates)):
                    s2 = s2_ref[h, r0 + q * V7X_SUBLANES:r0 + (q + 1) * V7X_SUBLANES, :]
                    term = jnp.where(s2 > thr, jnp.exp2(s2 + c), 0.0)
                    gates[q] = term if gates[q] is None else gates[q] + term
            gate = jnp.concatenate(gates, axis=0)
            rows = slice(ii * n_keys + r0, ii * n_keys + r0 + PEER_GATE_ROWS)
            at_ref[slot, rows, :] = (_gelu(ht_ref[rows, :]) * gate).astype(_bf16)

    @pl.when(j == pl.num_programs(1) - 1)
    def _():
        for cb in range(0, d, V7X_LANES):
            cs = slice(cb, cb + V7X_LANES)
            o_ref[:, cs] = ALPHA * o_ref[:, cs] + acc_ref[cs, :].T
        _residual_layer_norm(o_ref, None, g_ref, b_ref)


def _peer_dense(x, u_bf, vt_bf, s2t, thrt, ct, g, b):
    N, D = x.shape
    E = u_bf.shape[0]
    n_keys = s2t.shape[1]
    tn = _tile(N, 512)
    te = _tile(E, 4 * n_keys)
    nj = E // te
    vec = pl.BlockSpec((1, D), lambda i, j: (0, 0))
    once = pl.Buffered(1)
    stat = pl.BlockSpec((PEER_HEADS, n_keys, tn), lambda i, j: (0, 0, i), pipeline_mode=once)
    return pl.pallas_call(
        _peer_dense_kernel,
        grid=(N // tn, nj + 1),
        in_specs=[pl.BlockSpec(memory_space=pl.ANY),
                  pl.BlockSpec((te, D), lambda i, j: (jnp.minimum(j, nj - 1), 0)),
                  pl.BlockSpec((D, te), lambda i, j: (0, jnp.maximum(j - 1, 0))),
                  stat, stat, stat, vec, vec],
        out_specs=pl.BlockSpec((tn, D), lambda i, j: (i, 0)),
        out_shape=jax.ShapeDtypeStruct((N, D), _f32),
        scratch_shapes=[pltpu.VMEM((D, tn), _bf16), pltpu.VMEM((D, tn), _f32),
                        pltpu.VMEM((te, tn), _f32), pltpu.VMEM((2, te, tn), _bf16),
                        pltpu.VMEM((te // n_keys, 2 * PEER_HEADS, V7X_SUBLANES, tn), _f32),
                        pltpu.SemaphoreType.DMA(())],
        compiler_params=_params(("parallel", "arbitrary")),
        name="peer_dense",
    )(x, u_bf, vt_bf, s2t, thrt, ct, g, b)


def _expert_tables_kernel(u_ref, v_ref, ub_ref, vt_ref):
    ub_ref[...] = u_ref[...].astype(_bf16)
    for cb in range(0, v_ref.shape[1], V7X_LANES):
        vt_ref[cb:cb + V7X_LANES, :] = v_ref[:, cb:cb + V7X_LANES].T.astype(_bf16)


def _expert_tables(peer_u, peer_v, layer):
    _, E, D = peer_u.shape
    te = _tile(E, 512, V7X_LANES)
    src = pl.BlockSpec((None, te, D), lambda j: (layer, j, 0))
    return pl.pallas_call(
        _expert_tables_kernel,
        grid=(E // te,),
        in_specs=[src, src],
        out_specs=[pl.BlockSpec((te, D), lambda j: (j, 0)), pl.BlockSpec((D, te), lambda j: (0, j))],
        out_shape=[jax.ShapeDtypeStruct((E, D), _bf16), jax.ShapeDtypeStruct((D, E), _bf16)],
        compiler_params=_params(("parallel",)),
        name="expert_tables",
    )(peer_u, peer_v)


def _peer_layer(x, wq_bf, keys, u_bf, vt_bf, g, b):
    s2t, thrt, ct = _peer_stats(x, wq_bf, keys)
    return _peer_dense(x, u_bf, vt_bf, s2t, thrt, ct, g, b)


def _rwkv_mix_body(x, xp, mu_ref, w1_ref, a1_ref, g1_ref,
                   xr_ref, xk_ref, xv_ref, hw_ref, ha_ref, hg_ref):
    xx = xp - x

    def mix(i):
        return (x + xx * mu_ref[i:i + 1, :]).astype(_bf16)

    xr_ref[...] = mix(0)
    xk_ref[...] = mix(2)
    xv_ref[...] = mix(3)
    hw_ref[...] = jnp.tanh(_dot(mix(1), w1_ref[...])).astype(_bf16)
    ha_ref[...] = _dot(mix(4), a1_ref[...]).astype(_bf16)
    hg_ref[...] = (1.0 / (1.0 + jnp.exp(-_dot(mix(5), g1_ref[...])))).astype(_bf16)


def _rwkv_mix_kernel(x_ref, xp_ref, *rest):
    _rwkv_mix_body(x_ref[...], xp_ref[...], *rest)


def _rwkv_mix_shift_kernel(x_ref, halo_ref, *rest):
    *refs, ext_ref = rest
    tm = x_ref.shape[0]
    ext_ref[V7X_SUBLANES - 1:V7X_SUBLANES, :] = halo_ref[...]
    ext_ref[V7X_SUBLANES:V7X_SUBLANES + tm, :] = x_ref[...]
    _rwkv_mix_body(x_ref[...], ext_ref[V7X_SUBLANES - 1:V7X_SUBLANES - 1 + tm, :], *refs)


def _rwkv_mix(x, mu, w1_bf, a1_bf, g1_bf, *, xp=None, seq_len=None):
    N, D = x.shape
    tm = _tile(N if seq_len is None else seq_len, 256)
    row = pl.BlockSpec((tm, D), lambda i: (i, 0))

    def full(a):
        return pl.BlockSpec(a.shape, lambda i: (0, 0))

    def out(cols, dt):
        return pl.BlockSpec((tm, cols), lambda i: (i, 0)), jax.ShapeDtypeStruct((N, cols), dt)

    outs = [out(D, _bf16), out(D, _bf16), out(D, _bf16), out(w1_bf.shape[1], _bf16),
            out(a1_bf.shape[1], _bf16), out(g1_bf.shape[1], _bf16)]
    weights = [mu, w1_bf, a1_bf, g1_bf]
    if xp is not None:
        body, second, second_spec, scratch = _rwkv_mix_kernel, xp, row, []
    else:
        nt = N // tm
        last = x.reshape(nt, tm, D)[:, tm - 1]
        prev_last = jnp.concatenate([jnp.zeros((1, D), x.dtype), last[:-1]], axis=0)
        starts = (jnp.arange(nt) % (seq_len // tm)) == 0
        second = jnp.where(starts[:, None], 0.0, prev_last).reshape(nt, 1, D)
        second_spec = pl.BlockSpec((None, 1, D), lambda i: (i, 0, 0))
        body = _rwkv_mix_shift_kernel
        scratch = [pltpu.VMEM((V7X_SUBLANES + tm, D), _f32)]
    return pl.pallas_call(
        body,
        grid=(N // tm,),
        in_specs=[row, second_spec] + [full(w) for w in weights],
        out_specs=[o[0] for o in outs],
        out_shape=[o[1] for o in outs],
        scratch_shapes=scratch,
        compiler_params=_params(("parallel",)),
        name="rwkv_mix",
    )(x, second, *weights)


def _rwkv_proj_kernel(xr_ref, xk_ref, xv_ref, wr_ref, wk_ref, wv_ref, hw_ref, ha_ref, hg_ref,
                      w2_ref, a2_ref, g2_ref, w0_ref, a0_ref,
                      r_ref, lw_ref, k_ref, v_ref, a_ref, g_ref):
    r_ref[...] = _dot(xr_ref[...], wr_ref[...])
    k_ref[...] = _dot(xk_ref[...], wk_ref[...])
    v_ref[...] = _dot(xv_ref[...], wv_ref[...])
    u = -(w0_ref[...] + _dot(hw_ref[...], w2_ref[...]))
    softplus = jnp.maximum(u, 0.0) + jnp.log(1.0 + jnp.exp(-jnp.abs(u)))
    lw_ref[...] = -jnp.exp(-softplus - 0.5)
    a_ref[...] = 1.0 / (1.0 + jnp.exp(-(a0_ref[...] + _dot(ha_ref[...], a2_ref[...]))))
    g_ref[...] = _dot(hg_ref[...], g2_ref[...])


def _rwkv_proj(xr, xk, xv, wrkv_bf, hw, ha, hg, w2_bf, a2_bf, g2_bf, w0, a0):
    N, D = xr.shape
    tm = _tile(N, 512)
    tc = _tile(D, 256)
    row = pl.BlockSpec((tm, D), lambda i, j: (i, 0))

    def wmat(m):
        return pl.BlockSpec((None, D, tc), lambda i, j, m=m: (m, 0, j))

    def rowfull(a):
        return pl.BlockSpec((tm, a.shape[1]), lambda i, j: (i, 0))

    def colblk(a):
        return pl.BlockSpec((a.shape[0], tc), lambda i, j: (0, j))

    ospec = pl.BlockSpec((tm, tc), lambda i, j: (i, j))
    oshape = jax.ShapeDtypeStruct((N, D), _f32)
    return pl.pallas_call(
        _rwkv_proj_kernel,
        grid=(N // tm, D // tc),
        in_specs=[row, row, row, wmat(0), wmat(1), wmat(2), rowfull(hw), rowfull(ha), rowfull(hg),
                  colblk(w2_bf), colblk(a2_bf), colblk(g2_bf), colblk(w0), colblk(a0)],
        out_specs=[ospec] * 6,
        out_shape=[oshape] * 6,
        compiler_params=_params(("parallel", "arbitrary")),
        name="rwkv_proj",
    )(xr, xk, xv, wrkv_bf, wrkv_bf, wrkv_bf, hw, ha, hg, w2_bf, a2_bf, g2_bf, w0, a0)


def _wkv_kernel(r_ref, lw_ref, k_ref, v_ref, a_ref, s0_ref, kk_ref, ka_ref, rk_ref, gg_ref, gb_ref,
                y_ref, s_ref, an_ref, dr_ref, dec_ref, bn_ref, km_ref, sc_ref):
    tc = r_ref.shape[0]
    hd = r_ref.shape[1]
    ps = WKV_PREP_STEPS if tc % WKV_PREP_STEPS == 0 else tc

    @pl.when(pl.program_id(1) == 0)
    def _():
        s_ref[...] = s0_ref[...]

    def prep(i, carry):
        ts = pl.ds(pl.multiple_of(i * ps, ps), ps)
        r = r_ref[ts]
        k = k_ref[ts]
        a = a_ref[ts]
        dec = jnp.exp(lw_ref[ts])
        kk = k * kk_ref[...]
        kk = kk * lax.rsqrt(jnp.maximum(jnp.sum(kk * kk, axis=1, keepdims=True), 1e-24))
        kmod = k * (1.0 + (a - 1.0) * ka_ref[...])
        bn = kk * a
        an_ref[ts] = -kk
        dr_ref[ts] = dec * r
        dec_ref[ts] = dec
        bn_ref[ts] = bn
        km_ref[ts] = kmod
        sc_ref[0, ts, :] = jnp.sum(bn * r, axis=1)
        sc_ref[1, ts, :] = jnp.sum(kmod * r, axis=1)
        sc_ref[2, ts, :] = jnp.sum(r * kmod * rk_ref[...], axis=1)
        return carry

    lax.fori_loop(0, tc // ps, prep, 0)

    def step(t, carry):
        v = v_ref[t]
        row = pl.ds(t, 1)
        ju = WKV_KEY_UNROLL if hd % WKV_KEY_UNROLL == 0 else hd

        def reduce_keys(jc, acc):
            sa, z = acc
            for jj in range(ju):
                j = jc * ju + jj
                sj = s_ref[j]
                sa = sa + sj * an_ref[t, pl.ds(j, 1), :]
                z = z + sj * dr_ref[t, pl.ds(j, 1), :]
            return sa, z

        sa, z = lax.fori_loop(0, hd // ju, reduce_keys, (jnp.zeros_like(v), jnp.zeros_like(v)))
        y_ref[t] = z + sa * sc_ref[0, row, :] + v * sc_ref[1, row, :]

        def update_keys(jc, c2):
            for jj in range(ju):
                j = jc * ju + jj
                s_ref[j] = (s_ref[j] * dec_ref[t, pl.ds(j, 1), :] + sa * bn_ref[t, pl.ds(j, 1), :]
                            + v * km_ref[t, pl.ds(j, 1), :])
            return c2

        lax.fori_loop(0, hd // ju, update_keys, 0)
        return carry

    lax.fori_loop(0, tc, step, 0)

    def finish(i, carry):
        ts = pl.ds(pl.multiple_of(i * ps, ps), ps)
        y = y_ref[ts]
        m = jnp.mean(y, axis=1, keepdims=True)
        yc = y - m
        var = jnp.mean(yc * yc, axis=1, keepdims=True)
        yn = yc * lax.rsqrt(var + GN_EPS) * gg_ref[...] + gb_ref[...]
        y_ref[ts] = yn + sc_ref[2, ts, :][:, None, :] * v_ref[ts]
        return carry

    lax.fori_loop(0, tc // ps, finish, 0)


def _wkv(r, lw, k, v, a, s0, kk_t, ka_t, rk_t, gg_t, gb_t):
    T, hd, L = r.shape
    tl = _tile(L, V7X_LANES)
    tc = _tile(T, 32)
    op = pl.BlockSpec((tc, hd, tl), lambda g, c: (c, 0, g))
    st = pl.BlockSpec((hd, hd, tl), lambda g, c: (0, 0, g))
    par = pl.BlockSpec((hd, tl), lambda g, c: (0, g))
    return pl.pallas_call(
        _wkv_kernel,
        grid=(L // tl, T // tc),
        in_specs=[op, op, op, op, op, st, par, par, par, par, par],
        out_specs=[op, st],
        out_shape=[jax.ShapeDtypeStruct((T, hd, L), _f32), jax.ShapeDtypeStruct((hd, hd, L), _f32)],
        scratch_shapes=[pltpu.VMEM((tc, hd, tl), _f32)] * 5 + [pltpu.VMEM((3, tc, tl), _f32)],
        compiler_params=_params(("parallel", "arbitrary")),
        name="wkv_scan",
    )(r, lw, k, v, a, s0, kk_t, ka_t, rk_t, gg_t, gb_t)


def _wkv_seq_kernel(r_ref, lw_ref, k_ref, v_ref, a_ref, kk_ref, ka_ref, rk_ref, gg_ref, gb_ref,
                    y_ref, s_ref, an_ref, dr_ref, dec_ref, bn_ref, km_ref, vt_ref, yt_ref, sc_ref):
    nb, tc, d = r_ref.shape
    hd = kk_ref.shape[0]
    nh = d // hd
    ps = WKV_PREP_STEPS if tc % WKV_PREP_STEPS == 0 else tc
    ju = WKV_KEY_UNROLL if hd % WKV_KEY_UNROLL == 0 else hd

    @pl.when(pl.program_id(1) == 0)
    def _():
        s_ref[...] = jnp.zeros_like(s_ref)

    def slab(ref, j):
        return jnp.concatenate([ref[b, :, j * nh:(j + 1) * nh] for b in range(nb)], axis=-1)

    zero = jnp.zeros((tc, nb * nh), _f32)
    ss, q, kr, bonus = zero, zero, zero, zero
    for j in range(hd):
        r = slab(r_ref, j)
        k = slab(k_ref, j)
        a = slab(a_ref, j)
        dec = jnp.exp(slab(lw_ref, j))
        vt_ref[:, j, :] = slab(v_ref, j)
        kk = k * kk_ref[j:j + 1, :]
        kmod = k * (1.0 + (a - 1.0) * ka_ref[j:j + 1, :])
        kka = kk * a
        an_ref[j] = kk
        bn_ref[j] = kka
        dr_ref[j] = dec * r
        dec_ref[j] = dec
        km_ref[j] = kmod
        ss = ss + kk * kk
        q = q + kka * r
        kr = kr + kmod * r
        bonus = bonus + r * kmod * rk_ref[j:j + 1, :]
    scale = lax.rsqrt(jnp.maximum(ss, 1e-24))
    sc_ref[0] = -scale
    sc_ref[1] = scale
    sc_ref[2] = q * scale
    sc_ref[3] = kr
    sc_ref[4] = bonus

    def step(t, carry):
        row = pl.ds(t, 1)
        v = vt_ref[t]

        def reduce_keys(jc, acc):
            sa, z = acc
            for jj in range(ju):
                j = jc * ju + jj
                sj = s_ref[j]
                sa = sa + sj * an_ref[j, row, :]
                z = z + sj * dr_ref[j, row, :]
            return sa, z

        sa, z = lax.fori_loop(0, hd // ju, reduce_keys, (jnp.zeros_like(v), jnp.zeros_like(v)))
        sa = sa * sc_ref[0, row, :]
        yt_ref[t] = z + sa * sc_ref[2, row, :] + v * sc_ref[3, row, :]
        sb = sa * sc_ref[1, row, :]

        def update_keys(jc, c2):
            for jj in range(ju):
                j = jc * ju + jj
                s_ref[j] = (s_ref[j] * dec_ref[j, row, :] + sb * bn_ref[j, row, :]
                            + v * km_ref[j, row, :])
            return c2

        lax.fori_loop(0, hd // ju, update_keys, 0)
        return carry

    lax.fori_loop(0, tc, step, 0)

    def finish(i, carry):
        ts = pl.ds(pl.multiple_of(i * ps, ps), ps)
        y = yt_ref[ts]
        m = jnp.mean(y, axis=1, keepdims=True)
        yc = y - m
        var = jnp.mean(yc * yc, axis=1, keepdims=True)
        yn = yc * lax.rsqrt(var + GN_EPS) * gg_ref[...] + gb_ref[...]
        yt_ref[ts] = yn + sc_ref[4, ts, :][:, None, :] * vt_ref[ts]
        return carry

    lax.fori_loop(0, tc // ps, finish, 0)
    for i in range(hd):
        yi = yt_ref[:, i, :]
        for b in range(nb):
            y_ref[b, :, i * nh:(i + 1) * nh] = yi[:, b * nh:(b + 1) * nh]


def _wkv_seq(r, lw, k, v, a, kk_t, ka_t, rk_t, gg_t, gb_t):
    B, T, D = r.shape
    hd, L = kk_t.shape
    nb = L // (D // hd)
    tc = _tile(T, 32)
    op = pl.BlockSpec((nb, tc, D), lambda g, c: (g, c, 0))
    st = pl.BlockSpec((hd, hd, L), lambda g, c: (0, 0, g))
    par = pl.BlockSpec((hd, L), lambda g, c: (0, 0))
    slabs = pltpu.VMEM((hd, tc, L), _f32)
    tiles = pltpu.VMEM((tc, hd, L), _f32)
    return pl.pallas_call(
        _wkv_seq_kernel,
        grid=(B // nb, T // tc),
        in_specs=[op, op, op, op, op, par, par, par, par, par],
        out_specs=[op, st],
        out_shape=[jax.ShapeDtypeStruct((B, T, D), _f32),
                   jax.ShapeDtypeStruct((hd, hd, (B // nb) * L), _f32)],
        scratch_shapes=[slabs] * 5 + [tiles, tiles, pltpu.VMEM((5, tc, L), _f32)],
        compiler_params=_params(("parallel", "arbitrary")),
        name="wkv_seq",
    )(r, lw, k, v, a, kk_t, ka_t, rk_t, gg_t, gb_t)


def _rwkv_out_kernel(y_ref, g_ref, wo_ref, x_ref, lg_ref, lb_ref, o_ref):
    kt = pl.program_id(1)

    @pl.when(kt == 0)
    def _():
        o_ref[...] = jnp.zeros_like(o_ref)

    o_ref[...] += _dot((y_ref[...] * g_ref[...]).astype(_bf16), wo_ref[...])

    @pl.when(kt == pl.num_programs(1) - 1)
    def _():
        _residual_layer_norm(o_ref, x_ref, lg_ref, lb_ref)


def _rwkv_out(y, g, wo_bf, x, lg, lb):
    N, D = x.shape
    tm = _tile(N, 512)
    tk = _tile(D, 512)
    vec = pl.BlockSpec((1, D), lambda i, kt: (0, 0))
    return pl.pallas_call(
        _rwkv_out_kernel,
        grid=(N // tm, D // tk),
        in_specs=[pl.BlockSpec((tm, tk), lambda i, kt: (i, kt)),
                  pl.BlockSpec((tm, tk), lambda i, kt: (i, kt)),
                  pl.BlockSpec((tk, D), lambda i, kt: (kt, 0)),
                  pl.BlockSpec((tm, D), lambda i, kt: (i, 0)),
                  vec, vec],
        out_specs=pl.BlockSpec((tm, D), lambda i, kt: (i, 0)),
        out_shape=jax.ShapeDtypeStruct((N, D), _f32),
        compiler_params=_params(("parallel", "arbitrary")),
        name="rwkv_out",
    )(y, g, wo_bf, x, lg, lb)


def _to_lanes(a, steps, batch, hd):
    H = a.shape[-1] // hd
    return a.reshape(steps, batch, hd, H).transpose(0, 2, 1, 3).reshape(steps, hd, batch * H)


def _from_lanes(a, steps, batch, hd):
    H = a.shape[-1] // batch
    return a.reshape(steps, hd, batch, H).transpose(0, 2, 1, 3).reshape(steps * batch, hd * H)


def _head_param(p, batch, hd):
    H = p.shape[-1] // hd
    return jnp.tile(p.reshape(H, hd).T[:, None, :], (1, batch, 1)).reshape(hd, batch * H)


def _perm_last(w, hd):
    H = w.shape[-1] // hd
    return w.reshape(w.shape[:-1] + (H, hd)).swapaxes(-1, -2).reshape(w.shape)


def _perm_first(w, hd):
    H = w.shape[0] // hd
    return w.reshape((H, hd) + w.shape[1:]).swapaxes(0, 1).reshape(w.shape)


def kernel(x_prompt, x_sample, state_pool, state_rwkv_shift, state_rwkv_wkv, pool_w, pool_scale,
           rwkv_mu, rwkv_w_rkv, rwkv_w0, rwkv_w1, rwkv_w2, rwkv_a0, rwkv_a1, rwkv_a2, rwkv_g1,
           rwkv_g2, rwkv_k_k, rwkv_k_a, rwkv_r_k, rwkv_lnx_g, rwkv_lnx_b, rwkv_w_o, peer_w_q,
           peer_keys, peer_u, peer_v, ln_g, ln_b):
    B, T, D = x_prompt.shape
    Bs, Ts, _ = x_sample.shape
    hd = RWKV_HEAD
    H = D // hd
    n_p = B * T
    bf = lambda a: a.astype(_bf16)
    row = lambda a: a.reshape(1, D)

    pw_hi = bf(pool_w[0])
    pw = jnp.concatenate([pw_hi, bf(pool_w[0] - pw_hi.astype(_f32))], axis=0)
    ps = row(pool_scale[0])
    g00, b00 = row(ln_g[0, 0]), row(ln_b[0, 0])
    past_p = jnp.zeros((B, POOL_BUF, D), x_prompt.dtype)
    x1p = _pool_prompt(x_prompt, past_p, 0, pw, ps, g00, b00).reshape(n_p, D)
    ext_s = jnp.concatenate([state_pool[0], x_sample], axis=1)
    x1s = _pool_sample(ext_s.transpose(1, 0, 2), PAST_LEN, pw, ps, g00, b00).reshape(Ts * Bs, D)
    pool_p = x_prompt[:, T - POOL_BUF:][None]
    pool_s = ext_s[:, Ts:][None]

    peer0 = (peer_w_q[0], peer_keys[0], *_expert_tables(peer_u, peer_v, 0),
             row(ln_g[0, 1]), row(ln_b[0, 1]))
    x2p = _peer_layer(x1p, *peer0)
    x2s = _peer_layer(x1s, *peer0)
    shift_p = x2p.reshape(B, T, D)[:, -1][None]
    shift_s = x2s.reshape(Ts, Bs, D)[-1][None]

    g1 = rwkv_g1[0]
    g2 = rwkv_g2[0]
    gpad = (-g1.shape[1]) % V7X_LANES
    g1 = jnp.pad(g1, ((0, 0), (0, gpad)))
    g2 = jnp.pad(g2, ((0, gpad), (0, 0)))
    mixw = (rwkv_mu[0], bf(rwkv_w1[0]), bf(rwkv_a1[0]), bf(g1))
    pc = lambda w: _perm_last(w, hd)
    projw = (bf(pc(rwkv_w_rkv[0])), bf(pc(rwkv_w2[0])), bf(pc(rwkv_a2[0])), bf(pc(g2)),
             row(pc(rwkv_w0[0])), row(pc(rwkv_a0[0])))
    wo_bf = bf(_perm_first(rwkv_w_o[0], hd))
    head_pars = (rwkv_k_k[0], rwkv_k_a[0], rwkv_r_k[0].reshape(D), rwkv_lnx_g[0], rwkv_lnx_b[0])
    lg10, lb10 = row(ln_g[1, 0]), row(ln_b[1, 0])

    def project(x, **shift):
        xr, xk, xv, hw, ha, hg = _rwkv_mix(x, *mixw, **shift)
        return _rwkv_proj(xr, xk, xv, projw[0], hw, ha, hg, *projw[1:])

    r, lw, k, v, a, g = project(x2p, seq_len=T)
    nb = 2 if B % 2 == 0 else 1
    seq = lambda o: o.reshape(B, T, D)
    y_p, s_p = _wkv_seq(seq(r), seq(lw), seq(k), seq(v), seq(a),
                        *[_head_param(p, nb, hd) for p in head_pars])
    wkv_p = s_p.reshape(hd, hd, B, H).transpose(2, 3, 1, 0)
    x3p = _rwkv_out(y_p.reshape(n_p, D), g, wo_bf, x2p, lg10, lb10)

    xprev_s = jnp.concatenate([state_rwkv_shift[0], x2s[:(Ts - 1) * Bs]], axis=0)
    r, lw, k, v, a, g = project(x2s, xp=xprev_s)
    s0_s = state_rwkv_wkv[0].transpose(3, 2, 0, 1).reshape(hd, hd, Bs * H)
    y_s, s_s = _wkv(*[_to_lanes(o, Ts, Bs, hd) for o in (r, lw, k, v, a)], s0_s,
                    *[_head_param(p, Bs, hd) for p in head_pars])
    wkv_s = s_s.reshape(hd, hd, Bs, H).transpose(2, 3, 1, 0)
    x3s = _rwkv_out(_from_lanes(y_s, Ts, Bs, hd), g, wo_bf, x2s, lg10, lb10)

    peer1 = (bf(peer_w_q[1]), peer_keys[1], *_expert_tables(peer_u, peer_v, 1),
             row(ln_g[1, 1]), row(ln_b[1, 1]))
    y_prompt = _peer_layer(x3p, *peer1).reshape(B, T, D)
    y_sample = _peer_layer(x3s, *peer1).reshape(Ts, Bs, D).transpose(1, 0, 2)
    return (y_prompt, y_sample, pool_p, pool_s, shift_p, shift_s, wkv_p[None], wkv_s[None])
```

```python
import functools
import math

import jax
import jax.numpy as jnp
from jax import lax
from jax.experimental import pallas as pl
from jax.experimental.pallas import tpu as pltpu

POOL_WINDOWS = (2, 4, 8, 16)
POOL_BUF = max(POOL_WINDOWS) - 1
POOL_HALO = POOL_BUF + 1
RWKV_HEAD = 64
GN_EPS = 64e-5
PEER_HEADS = 8
PEER_TOPK = 16
PEER_GATE_ROWS = 16
LN_EPS = 1e-5
DEPTH = 2
ALPHA = (2.0 * DEPTH) ** 0.25
PAST_LEN = 16384

V7X_LANES = 128
V7X_SUBLANES = 8
LN_ROWS = 64
WKV_PREP_STEPS = 8
WKV_KEY_UNROLL = 8
V7X_VMEM_LIMIT = 63 * 1024 * 1024
NEG_BIG = -3.0e38
LOG2E = 1.4426950408889634

_f32 = jnp.float32
_bf16 = jnp.bfloat16


def _tile(dim, pref, align=8):
    if dim <= pref:
        return dim
    for t in range(pref - pref % align, 0, -align):
        if dim % t == 0:
            return t
    raise ValueError((dim, pref, align))


def _params(sem):
    return pltpu.CompilerParams(dimension_semantics=sem, vmem_limit_bytes=V7X_VMEM_LIMIT)


def _layer_norm(z, g, b):
    mu = jnp.mean(z, axis=-1, keepdims=True)
    zc = z - mu
    var = jnp.mean(zc * zc, axis=-1, keepdims=True)
    return zc * lax.rsqrt(var + LN_EPS) * g + b


def _residual_layer_norm(o_ref, x_ref, g_ref, b_ref):
    n_rows = o_ref.shape[0]
    rows_per = LN_ROWS if n_rows % LN_ROWS == 0 else n_rows
    g = g_ref[...]
    b = b_ref[...]

    def body(i, carry):
        rows = pl.ds(pl.multiple_of(i * rows_per, rows_per), rows_per)
        z = o_ref[rows, :]
        if x_ref is not None:
            z = ALPHA * x_ref[rows, :] + z
        o_ref[rows, :] = _layer_norm(z, g, b)
        return carry

    lax.fori_loop(0, n_rows // rows_per, body, 0)


def _dot(a, b, precision=None):
    return jnp.dot(a, b, preferred_element_type=_f32, precision=precision)


def _dot_nt(a, b, precision=None):
    return lax.dot_general(a, b, (((1,), (1,)), ((), ())), preferred_element_type=_f32,
                           precision=precision)


def _dot_tn(a, b, precision=None):
    return lax.dot_general(a, b, (((0,), (0,)), ((), ())), preferred_element_type=_f32,
                           precision=precision)


def _pool_finish(x, window_sum, n_real, w_ref, scale_ref, g_ref, b_ref, z_ref):
    gc = x.shape[-1] // len(POOL_WINDOWS)
    for gi, w in enumerate(POOL_WINDOWS):
        cs = slice(gi * gc, (gi + 1) * gc)
        if isinstance(n_real, int):
            inv_cnt = 1.0 / float(min(n_real, w))
        else:
            inv_cnt = 1.0 / jnp.minimum(n_real, w).astype(_f32)
        d = window_sum(gi, w) * inv_cnt - x[:, cs]
        dh = d.astype(_bf16)
        dl = (d - dh.astype(_f32)).astype(_bf16)
        wh, wl = w_ref[gi], w_ref[len(POOL_WINDOWS) + gi]
        y = (_dot(dh, wh) + (_dot(dl, wh) + _dot(dh, wl))) * scale_ref[:, cs]
        z_ref[:, cs] = ALPHA * x[:, cs] + y
    _residual_layer_norm(z_ref, None, g_ref, b_ref)


def _pool_prompt_kernel(x_ref, halo_ref, w_ref, scale_ref, g_ref, b_ref, o_ref, ext_ref, *, start):
    tt = x_ref.shape[0]
    gc = x_ref.shape[1] // len(POOL_WINDOWS)
    ext_ref[0:POOL_HALO, :] = halo_ref[...]
    ext_ref[POOL_HALO:POOL_HALO + tt, :] = x_ref[...]
    t0 = pl.program_id(1) * tt
    n_real = lax.broadcasted_iota(jnp.int32, (tt, 1), 0) + (t0 + start + 1)

    def window_sum(gi, w):
        cs = slice(gi * gc, (gi + 1) * gc)
        s = ext_ref[POOL_HALO:POOL_HALO + tt, cs]
        for k in range(1, w):
            s = s + ext_ref[POOL_HALO - k:POOL_HALO - k + tt, cs]
        return s

    _pool_finish(x_ref[...], window_sum, n_real, w_ref, scale_ref, g_ref, b_ref, o_ref)


def _pool_prompt(x, past, start, w_bf, scale, g, b):
    B, T, D = x.shape
    tt = _tile(T, 256)
    nt = T // tt
    gc = D // len(POOL_WINDOWS)
    halo = jnp.concatenate([jnp.zeros((B, 1, D), x.dtype), past], axis=1)[:, None]
    if nt > 1:
        halo = jnp.concatenate([halo, x.reshape(B, nt, tt, D)[:, :-1, tt - POOL_HALO:]], axis=1)
    vec = pl.BlockSpec((1, D), lambda bi, ti: (0, 0))
    return pl.pallas_call(
        functools.partial(_pool_prompt_kernel, start=start),
        grid=(B, nt),
        in_specs=[pl.BlockSpec((None, tt, D), lambda bi, ti: (bi, ti, 0)),
                  pl.BlockSpec((None, None, POOL_HALO, D), lambda bi, ti: (bi, ti, 0, 0)),
                  pl.BlockSpec(w_bf.shape, lambda bi, ti: (0, 0, 0), pipeline_mode=pl.Buffered(1)),
                  vec, vec, vec],
        out_specs=pl.BlockSpec((None, tt, D), lambda bi, ti: (bi, ti, 0)),
        out_shape=jax.ShapeDtypeStruct((B, T, D), _f32),
        scratch_shapes=[pltpu.VMEM((POOL_HALO + tt, D), _f32)],
        compiler_params=_params(("parallel", "arbitrary")),
        name="pool_prompt",
    )(x, halo, w_bf, scale, g, b)


def _pool_sample_kernel(ext_ref, w_ref, scale_ref, g_ref, b_ref, o_ref, z_ref, *, start):
    ts = o_ref.shape[0]
    gc = o_ref.shape[2] // len(POOL_WINDOWS)
    for t in range(ts):
        def window_sum(gi, w, t=t):
            cs = slice(gi * gc, (gi + 1) * gc)
            s = ext_ref[POOL_BUF + t, :, cs]
            for k in range(1, w):
                s = s + ext_ref[POOL_BUF + t - k, :, cs]
            return s

        _pool_finish(ext_ref[POOL_BUF + t], window_sum, start + t + 1, w_ref, scale_ref,
                     g_ref, b_ref, z_ref)
        o_ref[t] = z_ref[...]


def _pool_sample(ext_tm, start, w_bf, scale, g, b):
    te, Bs, D = ext_tm.shape
    ts = te - POOL_BUF
    bb = _tile(Bs, 16)
    gc = D // len(POOL_WINDOWS)
    vec = pl.BlockSpec((1, D), lambda bi: (0, 0))
    return pl.pallas_call(
        functools.partial(_pool_sample_kernel, start=start),
        grid=(Bs // bb,),
        in_specs=[pl.BlockSpec((te, bb, D), lambda bi: (0, bi, 0)),
                  pl.BlockSpec(w_bf.shape, lambda bi: (0, 0, 0)),
                  vec, vec, vec],
        out_specs=pl.BlockSpec((ts, bb, D), lambda bi: (0, bi, 0)),
        out_shape=jax.ShapeDtypeStruct((ts, Bs, D), _f32),
        scratch_shapes=[pltpu.VMEM((bb, D), _f32)],
        compiler_params=_params(("parallel",)),
        name="pool_sample",
    )(ext_tm, w_bf, scale, g, b)


def _top_rows(s, n):
    rows = [jnp.max(s, axis=0, keepdims=True)]
    for _ in range(n - 1):
        rows.append(jnp.max(jnp.where(s < rows[-1], s, NEG_BIG), axis=0, keepdims=True))
    return rows


def _peer_stats_kernel(x_ref, wq_ref, wql_ref, keys_ref, s2_ref, thr_ref, c_ref, xh_ref, xl_ref,
                       *, split):
    half = keys_ref.shape[2]

    @pl.when(pl.program_id(1) == 0)
    def _():
        x = x_ref[...]
        xh = x.astype(_bf16)
        xh_ref[...] = xh
        if split:
            xl_ref[...] = (x - xh.astype(_f32)).astype(_bf16)

    q = _dot(xh_ref[...], wq_ref[...])
    if split:
        q = q + (_dot(xl_ref[...], wq_ref[...]) + _dot(xh_ref[...], wql_ref[...]))
    s1 = _dot_nt(keys_ref[0], q[:, :half], precision=lax.Precision.HIGHEST)
    s2 = _dot_nt(keys_ref[1], q[:, half:], precision=lax.Precision.HIGHEST)
    n = PEER_TOPK + 1
    t1 = _top_rows(s1, n)
    t2 = _top_rows(s2, n)
    cands = [t1[a] + t2[b] for a in range(n) for b in range(n) if (a + 1) * (b + 1) <= n]
    cands += [jnp.full_like(cands[0], NEG_BIG)] * ((-len(cands)) % V7X_SUBLANES)
    vals = _top_rows(jnp.concatenate(cands, axis=0), n)
    c0 = vals[0]
    z = jnp.zeros_like(c0)
    for r in range(PEER_TOPK):
        z = z + jnp.exp(vals[r] - c0)
    tau = 0.5 * (vals[PEER_TOPK - 1] + vals[PEER_TOPK])
    s2_ref[...] = s2 * LOG2E
    thr_ref[...] = (tau - s1) * LOG2E
    c_ref[...] = (s1 - (c0 + jnp.log(z))) * LOG2E


def _peer_stats(x, wq, keys):
    N, D = x.shape
    n_keys, half = keys.shape[1], keys.shape[2]
    tn = _tile(N, 512)
    split = wq.dtype == _f32
    wq_hi = wq.astype(_bf16)
    wq_lo = (wq - wq_hi.astype(_f32)).astype(_bf16) if split else wq_hi
    out = jax.ShapeDtypeStruct((PEER_HEADS, n_keys, N), _f32)
    ospec = pl.BlockSpec((None, n_keys, tn), lambda i, h: (h, 0, i))
    wspec = pl.BlockSpec((D, 2 * half), lambda i, h: (0, h))
    half_tile = pltpu.VMEM((tn, D), _bf16)
    return pl.pallas_call(
        functools.partial(_peer_stats_kernel, split=split),
        grid=(N // tn, PEER_HEADS),
        in_specs=[pl.BlockSpec((tn, D), lambda i, h: (i, 0)), wspec, wspec,
                  pl.BlockSpec((2, n_keys, half), lambda i, h: (0, 0, 0))],
        out_specs=[ospec, ospec, ospec],
        out_shape=[out, out, out],
        scratch_shapes=[half_tile, half_tile],
        compiler_params=_params(("parallel", "arbitrary")),
        name="peer_stats",
    )(x, wq_hi, wq_lo, keys)


def _gelu(h):
    return 0.5 * h * (1.0 + lax.erf(h * (1.0 / math.sqrt(2.0))))


def _peer_dense_kernel(x_hbm, u_ref, vt_ref, s2_ref, thr_ref, c_ref, g_ref, b_ref, o_ref,
                       xt_ref, acc_ref, ht_ref, at_ref, bc_ref, sem):
    i = pl.program_id(0)
    j = pl.program_id(1)
    n_keys = s2_ref.shape[1]
    te = u_ref.shape[0]
    tn, d = o_ref.shape
    sub = te // n_keys
    drow = d // sub
    slot = j % 2

    @pl.when(j == 0)
    def _():
        cp = pltpu.make_async_copy(x_hbm.at[pl.ds(pl.multiple_of(i * tn, tn), tn), :], o_ref, sem)
        cp.start()
        cp.wait()
        for cb in range(0, d, V7X_LANES):
            xt_ref[cb:cb + V7X_LANES, :] = o_ref[:, cb:cb + V7X_LANES].T.astype(_bf16)
        acc_ref[...] = jnp.zeros_like(acc_ref)
        at_ref[1] = jnp.zeros((te, tn), _bf16)

    ht_ref[...] = _dot(u_ref[...], xt_ref[...])
    a_prev = at_ref[1 - slot]
    for ii in range(sub):
        drows = slice(ii * drow, (ii + 1) * drow)
        acc_ref[drows, :] += _dot(vt_ref[drows, :], a_prev)
        i1 = jnp.minimum(j * sub + ii, n_keys - 1)
        for h in range(PEER_HEADS):
            bc_ref[ii, h] = jnp.broadcast_to(thr_ref[h, pl.ds(i1, 1), :], (V7X_SUBLANES, tn))
            bc_ref[ii, PEER_HEADS + h] = jnp.broadcast_to(c_ref[h, pl.ds(i1, 1), :],
                                                          (V7X_SUBLANES, tn))
        for r0 in range(0, n_keys, PEER_GATE_ROWS):
            gates = [None] * (PEER_GATE_ROWS // V7X_SUBLANES)
            for h in range(PEER_HEADS):
                thr = bc_ref[ii, h]
                c = bc_ref[ii, PEER_HEADS + h]
                for q in range(len(gates)):
                    s2 = s2_ref[h, r0 + q * V7X_SUBLANES:r0 + (q + 1) * V7X_SUBLANES, :]
                    term = jnp.where(s2 > thr, jnp.exp2(s2 + c), 0.0)
                    gates[q] = term if gates[q] is None else gates[q] + term
            gate = jnp.concatenate(gates, axis=0)
            rows = slice(ii * n_keys + r0, ii * n_keys + r0 + PEER_GATE_ROWS)
            at_ref[slot, rows, :] = (_gelu(ht_ref[rows, :]) * gate).astype(_bf16)

    @pl.when(j == pl.num_programs(1) - 1)
    def _():
        for cb in range(0, d, V7X_LANES):
            cs = slice(cb, cb + V7X_LANES)
            o_ref[:, cs] = ALPHA * o_ref[:, cs] + acc_ref[cs, :].T
        _residual_layer_norm(o_ref, None, g_ref, b_ref)


def _peer_dense(x, u_bf, vt_bf, s2t, thrt, ct, g, b):
    N, D = x.shape
    E = u_bf.shape[0]
    n_keys = s2t.shape[1]
    tn = _tile(N, 512)
    te = _tile(E, 4 * n_keys)
    nj = E // te
    vec = pl.BlockSpec((1, D), lambda i, j: (0, 0))
    once = pl.Buffered(1)
    stat = pl.BlockSpec((PEER_HEADS, n_keys, tn), lambda i, j: (0, 0, i), pipeline_mode=once)
    return pl.pallas_call(
        _peer_dense_kernel,
        grid=(N // tn, nj + 1),
        in_specs=[pl.BlockSpec(memory_space=pl.ANY),
                  pl.BlockSpec((te, D), lambda i, j: (jnp.minimum(j, nj - 1), 0)),
                  pl.BlockSpec((D, te), lambda i, j: (0, jnp.maximum(j - 1, 0))),
                  stat, stat, stat, vec, vec],
        out_specs=pl.BlockSpec((tn, D), lambda i, j: (i, 0)),
        out_shape=jax.ShapeDtypeStruct((N, D), _f32),
        scratch_shapes=[pltpu.VMEM((D, tn), _bf16), pltpu.VMEM((D, tn), _f32),
                        pltpu.VMEM((te, tn), _f32), pltpu.VMEM((2, te, tn), _bf16),
                        pltpu.VMEM((te // n_keys, 2 * PEER_HEADS, V7X_SUBLANES, tn), _f32),
                        pltpu.SemaphoreType.DMA(())],
        compiler_params=_params(("parallel", "arbitrary")),
        name="peer_dense",
    )(x, u_bf, vt_bf, s2t, thrt, ct, g, b)


def _expert_tables_kernel(u_ref, v_ref, ub_ref, vt_ref):
    ub_ref[...] = u_ref[...].astype(_bf16)
    for cb in range(0, v_ref.shape[1], V7X_LANES):
        vt_ref[cb:cb + V7X_LANES, :] = v_ref[:, cb:cb + V7X_LANES].T.astype(_bf16)


def _expert_tables(peer_u, peer_v, layer):
    _, E, D = peer_u.shape
    te = _tile(E, 512, V7X_LANES)
    src = pl.BlockSpec((None, te, D), lambda j: (layer, j, 0))
    return pl.pallas_call(
        _expert_tables_kernel,
        grid=(E // te,),
        in_specs=[src, src],
        out_specs=[pl.BlockSpec((te, D), lambda j: (j, 0)), pl.BlockSpec((D, te), lambda j: (0, j))],
        out_shape=[jax.ShapeDtypeStruct((E, D), _bf16), jax.ShapeDtypeStruct((D, E), _bf16)],
        compiler_params=_params(("parallel",)),
        name="expert_tables",
    )(peer_u, peer_v)


def _peer_layer(x, wq_bf, keys, u_bf, vt_bf, g, b):
    s2t, thrt, ct = _peer_stats(x, wq_bf, keys)
    return _peer_dense(x, u_bf, vt_bf, s2t, thrt, ct, g, b)


def _rwkv_mix_body(x, xp, mu_ref, w1_ref, a1_ref, g1_ref,
                   xr_ref, xk_ref, xv_ref, hw_ref, ha_ref, hg_ref):
    xx = xp - x

    def mix(i):
        return (x + xx * mu_ref[i:i + 1, :]).astype(_bf16)

    xr_ref[...] = mix(0)
    xk_ref[...] = mix(2)
    xv_ref[...] = mix(3)
    hw_ref[...] = jnp.tanh(_dot(mix(1), w1_ref[...])).astype(_bf16)
    ha_ref[...] = _dot(mix(4), a1_ref[...]).astype(_bf16)
    hg_ref[...] = (1.0 / (1.0 + jnp.exp(-_dot(mix(5), g1_ref[...])))).astype(_bf16)


def _rwkv_mix_kernel(x_ref, xp_ref, *rest):
    _rwkv_mix_body(x_ref[...], xp_ref[...], *rest)


def _rwkv_mix_shift_kernel(x_ref, halo_ref, *rest):
    *refs, ext_ref = rest
    tm = x_ref.shape[0]
    ext_ref[V7X_SUBLANES - 1:V7X_SUBLANES, :] = halo_ref[...]
    ext_ref[V7X_SUBLANES:V7X_SUBLANES + tm, :] = x_ref[...]
    _rwkv_mix_body(x_ref[...], ext_ref[V7X_SUBLANES - 1:V7X_SUBLANES - 1 + tm, :], *refs)


def _rwkv_mix(x, mu, w1_bf, a1_bf, g1_bf, *, xp=None, seq_len=None):
    N, D = x.shape
    tm = _tile(N if seq_len is None else seq_len, 256)
    row = pl.BlockSpec((tm, D), lambda i: (i, 0))

    def full(a):
        return pl.BlockSpec(a.shape, lambda i: (0, 0))

    def out(cols, dt):
        return pl.BlockSpec((tm, cols), lambda i: (i, 0)), jax.ShapeDtypeStruct((N, cols), dt)

    outs = [out(D, _bf16), out(D, _bf16), out(D, _bf16), out(w1_bf.shape[1], _bf16),
            out(a1_bf.shape[1], _bf16), out(g1_bf.shape[1], _bf16)]
    weights = [mu, w1_bf, a1_bf, g1_bf]
    if xp is not None:
        body, second, second_spec, scratch = _rwkv_mix_kernel, xp, row, []
    else:
        nt = N // tm
        last = x.reshape(nt, tm, D)[:, tm - 1]
        prev_last = jnp.concatenate([jnp.zeros((1, D), x.dtype), last[:-1]], axis=0)
        starts = (jnp.arange(nt) % (seq_len // tm)) == 0
        second = jnp.where(starts[:, None], 0.0, prev_last).reshape(nt, 1, D)
        second_spec = pl.BlockSpec((None, 1, D), lambda i: (i, 0, 0))
        body = _rwkv_mix_shift_kernel
        scratch = [pltpu.VMEM((V7X_SUBLANES + tm, D), _f32)]
    return pl.pallas_call(
        body,
        grid=(N // tm,),
        in_specs=[row, second_spec] + [full(w) for w in weights],
        out_specs=[o[0] for o in outs],
        out_shape=[o[1] for o in outs],
        scratch_shapes=scratch,
        compiler_params=_params(("parallel",)),
        name="rwkv_mix",
    )(x, second, *weights)


def _rwkv_proj_kernel(xr_ref, xk_ref, xv_ref, wr_ref, wk_ref, wv_ref, hw_ref, ha_ref, hg_ref,
                      w2_ref, a2_ref, g2_ref, w0_ref, a0_ref,
                      r_ref, lw_ref, k_ref, v_ref, a_ref, g_ref):
    r_ref[...] = _dot(xr_ref[...], wr_ref[...])
    k_ref[...] = _dot(xk_ref[...], wk_ref[...])
    v_ref[...] = _dot(xv_ref[...], wv_ref[...])
    u = -(w0_ref[...] + _dot(hw_ref[...], w2_ref[...]))
    softplus = jnp.maximum(u, 0.0) + jnp.log(1.0 + jnp.exp(-jnp.abs(u)))
    lw_ref[...] = -jnp.exp(-softplus - 0.5)
    a_ref[...] = 1.0 / (1.0 + jnp.exp(-(a0_ref[...] + _dot(ha_ref[...], a2_ref[...]))))
    g_ref[...] = _dot(hg_ref[...], g2_ref[...])


def _rwkv_proj(xr, xk, xv, wrkv_bf, hw, ha, hg, w2_bf, a2_bf, g2_bf, w0, a0):
    N, D = xr.shape
    tm = _tile(N, 512)
    tc = _tile(D, 256)
    row = pl.BlockSpec((tm, D), lambda i, j: (i, 0))

    def wmat(m):
        return pl.BlockSpec((None, D, tc), lambda i, j, m=m: (m, 0, j))

    def rowfull(a):
        return pl.BlockSpec((tm, a.shape[1]), lambda i, j: (i, 0))

    def colblk(a):
        return pl.BlockSpec((a.shape[0], tc), lambda i, j: (0, j))

    ospec = pl.BlockSpec((tm, tc), lambda i, j: (i, j))
    oshape = jax.ShapeDtypeStruct((N, D), _f32)
    return pl.pallas_call(
        _rwkv_proj_kernel,
        grid=(N // tm, D // tc),
        in_specs=[row, row, row, wmat(0), wmat(1), wmat(2), rowfull(hw), rowfull(ha), rowfull(hg),
                  colblk(w2_bf), colblk(a2_bf), colblk(g2_bf), colblk(w0), colblk(a0)],
        out_specs=[ospec] * 6,
        out_shape=[oshape] * 6,
        compiler_params=_params(("parallel", "arbitrary")),
        name="rwkv_proj",
    )(xr, xk, xv, wrkv_bf, wrkv_bf, wrkv_bf, hw, ha, hg, w2_bf, a2_bf, g2_bf, w0, a0)


def _wkv_kernel(r_ref, lw_ref, k_ref, v_ref, a_ref, s0_ref, kk_ref, ka_ref, rk_ref, gg_ref, gb_ref,
                y_ref, s_ref, an_ref, dr_ref, dec_ref, bn_ref, km_ref, sc_ref):
    tc = r_ref.shape[0]
    hd = r_ref.shape[1]
    ps = WKV_PREP_STEPS if tc % WKV_PREP_STEPS == 0 else tc

    @pl.when(pl.program_id(1) == 0)
    def _():
        s_ref[...] = s0_ref[...]

    def prep(i, carry):
        ts = pl.ds(pl.multiple_of(i * ps, ps), ps)
        r = r_ref[ts]
        k = k_ref[ts]
        a = a_ref[ts]
        dec = jnp.exp(lw_ref[ts])
        kk = k * kk_ref[...]
        kk = kk * lax.rsqrt(jnp.maximum(jnp.sum(kk * kk, axis=1, keepdims=True), 1e-24))
        kmod = k * (1.0 + (a - 1.0) * ka_ref[...])
        bn = kk * a
        an_ref[ts] = -kk
        dr_ref[ts] = dec * r
        dec_ref[ts] = dec
        bn_ref[ts] = bn
        km_ref[ts] = kmod
        sc_ref[0, ts, :] = jnp.sum(bn * r, axis=1)
        sc_ref[1, ts, :] = jnp.sum(kmod * r, axis=1)
        sc_ref[2, ts, :] = jnp.sum(r * kmod * rk_ref[...], axis=1)
        return carry

    lax.fori_loop(0, tc // ps, prep, 0)

    def step(t, carry):
        v = v_ref[t]
        row = pl.ds(t, 1)
        ju = WKV_KEY_UNROLL if hd % WKV_KEY_UNROLL == 0 else hd

        def reduce_keys(jc, acc):
            sa, z = acc
            for jj in range(ju):
                j = jc * ju + jj
                sj = s_ref[j]
                sa = sa + sj * an_ref[t, pl.ds(j, 1), :]
                z = z + sj * dr_ref[t, pl.ds(j, 1), :]
            return sa, z

        sa, z = lax.fori_loop(0, hd // ju, reduce_keys, (jnp.zeros_like(v), jnp.zeros_like(v)))
        y_ref[t] = z + sa * sc_ref[0, row, :] + v * sc_ref[1, row, :]

        def update_keys(jc, c2):
            for jj in range(ju):
                j = jc * ju + jj
                s_ref[j] = (s_ref[j] * dec_ref[t, pl.ds(j, 1), :] + sa * bn_ref[t, pl.ds(j, 1), :]
                            + v * km_ref[t, pl.ds(j, 1), :])
            return c2

        lax.fori_loop(0, hd // ju, update_keys, 0)
        return carry

    lax.fori_loop(0, tc, step, 0)

    def finish(i, carry):
        ts = pl.ds(pl.multiple_of(i * ps, ps), ps)
        y = y_ref[ts]
        m = jnp.mean(y, axis=1, keepdims=True)
        yc = y - m
        var = jnp.mean(yc * yc, axis=1, keepdims=True)
        yn = yc * lax.rsqrt(var + GN_EPS) * gg_ref[...] + gb_ref[...]
        y_ref[ts] = yn + sc_ref[2, ts, :][:, None, :] * v_ref[ts]
        return carry

    lax.fori_loop(0, tc // ps, finish, 0)


def _wkv(r, lw, k, v, a, s0, kk_t, ka_t, rk_t, gg_t, gb_t):
    T, hd, L = r.shape
    tl = _tile(L, V7X_LANES)
    tc = _tile(T, 32)
    op = pl.BlockSpec((tc, hd, tl), lambda g, c: (c, 0, g))
    st = pl.BlockSpec((hd, hd, tl), lambda g, c: (0, 0, g))
    par = pl.BlockSpec((hd, tl), lambda g, c: (0, g))
    return pl.pallas_call(
        _wkv_kernel,
        grid=(L // tl, T // tc),
        in_specs=[op, op, op, op, op, st, par, par, par, par, par],
        out_specs=[op, st],
        out_shape=[jax.ShapeDtypeStruct((T, hd, L), _f32), jax.ShapeDtypeStruct((hd, hd, L), _f32)],
        scratch_shapes=[pltpu.VMEM((tc, hd, tl), _f32)] * 5 + [pltpu.VMEM((3, tc, tl), _f32)],
        compiler_params=_params(("parallel", "arbitrary")),
        name="wkv_scan",
    )(r, lw, k, v, a, s0, kk_t, ka_t, rk_t, gg_t, gb_t)


def _wkv_seq_kernel(r_ref, lw_ref, k_ref, v_ref, a_ref, kk_ref, ka_ref, rk_ref, gg_ref, gb_ref,
                    y_ref, s_ref, an_ref, dr_ref, dec_ref, bn_ref, km_ref, vt_ref, yt_ref, sc_ref):
    nb, tc, d = r_ref.shape
    hd = kk_ref.shape[0]
    nh = d // hd
    ps = WKV_PREP_STEPS if tc % WKV_PREP_STEPS == 0 else tc
    ju = WKV_KEY_UNROLL if hd % WKV_KEY_UNROLL == 0 else hd

    @pl.when(pl.program_id(1) == 0)
    def _():
        s_ref[...] = jnp.zeros_like(s_ref)

    def slab(ref, j):
        return jnp.concatenate([ref[b, :, j * nh:(j + 1) * nh] for b in range(nb)], axis=-1)

    zero = jnp.zeros((tc, nb * nh), _f32)
    ss, q, kr, bonus = zero, zero, zero, zero
    for j in range(hd):
        r = slab(r_ref, j)
        k = slab(k_ref, j)
        a = slab(a_ref, j)
        dec = jnp.exp(slab(lw_ref, j))
        vt_ref[:, j, :] = slab(v_ref, j)
        kk = k * kk_ref[j:j + 1, :]
        kmod = k * (1.0 + (a - 1.0) * ka_ref[j:j + 1, :])
        kka = kk * a
        an_ref[j] = kk
        bn_ref[j] = kka
        dr_ref[j] = dec * r
        dec_ref[j] = dec
        km_ref[j] = kmod
        ss = ss + kk * kk
        q = q + kka * r
        kr = kr + kmod * r
        bonus = bonus + r * kmod * rk_ref[j:j + 1, :]
    scale = lax.rsqrt(jnp.maximum(ss, 1e-24))
    sc_ref[0] = -scale
    sc_ref[1] = scale
    sc_ref[2] = q * scale
    sc_ref[3] = kr
    sc_ref[4] = bonus

    def step(t, carry):
        row = pl.ds(t, 1)
        v = vt_ref[t]

        def reduce_keys(jc, acc):
            sa, z = acc
            for jj in range(ju):
                j = jc * ju + jj
                sj = s_ref[j]
                sa = sa + sj * an_ref[j, row, :]
                z = z + sj * dr_ref[j, row, :]
            return sa, z

        sa, z = lax.fori_loop(0, hd // ju, reduce_keys, (jnp.zeros_like(v), jnp.zeros_like(v)))
        sa = sa * sc_ref[0, row, :]
        yt_ref[t] = z + sa * sc_ref[2, row, :] + v * sc_ref[3, row, :]
        sb = sa * sc_ref[1, row, :]

        def update_keys(jc, c2):
            for jj in range(ju):
                j = jc * ju + jj
                s_ref[j] = (s_ref[j] * dec_ref[j, row, :] + sb * bn_ref[j, row, :]
                            + v * km_ref[j, row, :])
            return c2

        lax.fori_loop(0, hd // ju, update_keys, 0)
        return carry

    lax.fori_loop(0, tc, step, 0)

    def finish(i, carry):
        ts = pl.ds(pl.multiple_of(i * ps, ps), ps)
        y = yt_ref[ts]
        m = jnp.mean(y, axis=1, keepdims=True)
        yc = y - m
        var = jnp.mean(yc * yc, axis=1, keepdims=True)
        yn = yc * lax.rsqrt(var + GN_EPS) * gg_ref[...] + gb_ref[...]
        yt_ref[ts] = yn + sc_ref[4, ts, :][:, None, :] * vt_ref[ts]
        return carry

    lax.fori_loop(0, tc // ps, finish, 0)
    for i in range(hd):
        yi = yt_ref[:, i, :]
        for b in range(nb):
            y_ref[b, :, i * nh:(i + 1) * nh] = yi[:, b * nh:(b + 1) * nh]


def _wkv_seq(r, lw, k, v, a, kk_t, ka_t, rk_t, gg_t, gb_t):
    B, T, D = r.shape
    hd, L = kk_t.shape
    nb = L // (D // hd)
    tc = _tile(T, 32)
    op = pl.BlockSpec((nb, tc, D), lambda g, c: (g, c, 0))
    st = pl.BlockSpec((hd, hd, L), lambda g, c: (0, 0, g))
    par = pl.BlockSpec((hd, L), lambda g, c: (0, 0))
    slabs = pltpu.VMEM((hd, tc, L), _f32)
    tiles = pltpu.VMEM((tc, hd, L), _f32)
    return pl.pallas_call(
        _wkv_seq_kernel,
        grid=(B // nb, T // tc),
        in_specs=[op, op, op, op, op, par, par, par, par, par],
        out_specs=[op, st],
        out_shape=[jax.ShapeDtypeStruct((B, T, D), _f32),
                   jax.ShapeDtypeStruct((hd, hd, (B // nb) * L), _f32)],
        scratch_shapes=[slabs] * 5 + [tiles, tiles, pltpu.VMEM((5, tc, L), _f32)],
        compiler_params=_params(("parallel", "arbitrary")),
        name="wkv_seq",
    )(r, lw, k, v, a, kk_t, ka_t, rk_t, gg_t, gb_t)


def _rwkv_out_kernel(y_ref, g_ref, wo_ref, x_ref, lg_ref, lb_ref, o_ref):
    kt = pl.program_id(1)

    @pl.when(kt == 0)
    def _():
        o_ref[...] = jnp.zeros_like(o_ref)

    o_ref[...] += _dot((y_ref[...] * g_ref[...]).astype(_bf16), wo_ref[...])

    @pl.when(kt == pl.num_programs(1) - 1)
    def _():
        _residual_layer_norm(o_ref, x_ref, lg_ref, lb_ref)


def _rwkv_out(y, g, wo_bf, x, lg, lb):
    N, D = x.shape
    tm = _tile(N, 512)
    tk = _tile(D, 512)
    vec = pl.BlockSpec((1, D), lambda i, kt: (0, 0))
    return pl.pallas_call(
        _rwkv_out_kernel,
        grid=(N // tm, D // tk),
        in_specs=[pl.BlockSpec((tm, tk), lambda i, kt: (i, kt)),
                  pl.BlockSpec((tm, tk), lambda i, kt: (i, kt)),
                  pl.BlockSpec((tk, D), lambda i, kt: (kt, 0)),
                  pl.BlockSpec((tm, D), lambda i, kt: (i, 0)),
                  vec, vec],
        out_specs=pl.BlockSpec((tm, D), lambda i, kt: (i, 0)),
        out_shape=jax.ShapeDtypeStruct((N, D), _f32),
        compiler_params=_params(("parallel", "arbitrary")),
        name="rwkv_out",
    )(y, g, wo_bf, x, lg, lb)


def _to_lanes(a, steps, batch, hd):
    H = a.shape[-1] // hd
    return a.reshape(steps, batch, hd, H).transpose(0, 2, 1, 3).reshape(steps, hd, batch * H)


def _from_lanes(a, steps, batch, hd):
    H = a.shape[-1] // batch
    return a.reshape(steps, hd, batch, H).transpose(0, 2, 1, 3).reshape(steps * batch, hd * H)


def _head_param(p, batch, hd):
    H = p.shape[-1] // hd
    return jnp.tile(p.reshape(H, hd).T[:, None, :], (1, batch, 1)).reshape(hd, batch * H)


def _perm_last(w, hd):
    H = w.shape[-1] // hd
    return w.reshape(w.shape[:-1] + (H, hd)).swapaxes(-1, -2).reshape(w.shape)


def _perm_first(w, hd):
    H = w.shape[0] // hd
    return w.reshape((H, hd) + w.shape[1:]).swapaxes(0, 1).reshape(w.shape)


def kernel(x_prompt, x_sample, state_pool, state_rwkv_shift, state_rwkv_wkv, pool_w, pool_scale,
           rwkv_mu, rwkv_w_rkv, rwkv_w0, rwkv_w1, rwkv_w2, rwkv_a0, rwkv_a1, rwkv_a2, rwkv_g1,
           rwkv_g2, rwkv_k_k, rwkv_k_a, rwkv_r_k, rwkv_lnx_g, rwkv_lnx_b, rwkv_w_o, peer_w_q,
           peer_keys, peer_u, peer_v, ln_g, ln_b):
    B, T, D = x_prompt.shape
    Bs, Ts, _ = x_sample.shape
    hd = RWKV_HEAD
    H = D // hd
    n_p = B * T
    bf = lambda a: a.astype(_bf16)
    row = lambda a: a.reshape(1, D)

    pw_hi = bf(pool_w[0])
    pw = jnp.concatenate([pw_hi, bf(pool_w[0] - pw_hi.astype(_f32))], axis=0)
    ps = row(pool_scale[0])
    g00, b00 = row(ln_g[0, 0]), row(ln_b[0, 0])
    past_p = jnp.zeros((B, POOL_BUF, D), x_prompt.dtype)
    x1p = _pool_prompt(x_prompt, past_p, 0, pw, ps, g00, b00).reshape(n_p, D)
    ext_s = jnp.concatenate([state_pool[0], x_sample], axis=1)
    x1s = _pool_sample(ext_s.transpose(1, 0, 2), PAST_LEN, pw, ps, g00, b00).reshape(Ts * Bs, D)
    pool_p = x_prompt[:, T - POOL_BUF:][None]
    pool_s = ext_s[:, Ts:][None]

    peer0 = (peer_w_q[0], peer_keys[0], *_expert_tables(peer_u, peer_v, 0),
             row(ln_g[0, 1]), row(ln_b[0, 1]))
    x2p = _peer_layer(x1p, *peer0)
    x2s = _peer_layer(x1s, *peer0)
    shift_p = x2p.reshape(B, T, D)[:, -1][None]
    shift_s = x2s.reshape(Ts, Bs, D)[-1][None]

    g1 = rwkv_g1[0]
    g2 = rwkv_g2[0]
    gpad = (-g1.shape[1]) % V7X_LANES
    g1 = jnp.pad(g1, ((0, 0), (0, gpad)))
    g2 = jnp.pad(g2, ((0, gpad), (0, 0)))
    mixw = (rwkv_mu[0], bf(rwkv_w1[0]), bf(rwkv_a1[0]), bf(g1))
    pc = lambda w: _perm_last(w, hd)
    projw = (bf(pc(rwkv_w_rkv[0])), bf(pc(rwkv_w2[0])), bf(pc(rwkv_a2[0])), bf(pc(g2)),
             row(pc(rwkv_w0[0])), row(pc(rwkv_a0[0])))
    wo_bf = bf(_perm_first(rwkv_w_o[0], hd))
    head_pars = (rwkv_k_k[0], rwkv_k_a[0], rwkv_r_k[0].reshape(D), rwkv_lnx_g[0], rwkv_lnx_b[0])
    lg10, lb10 = row(ln_g[1, 0]), row(ln_b[1, 0])

    def project(x, **shift):
        xr, xk, xv, hw, ha, hg = _rwkv_mix(x, *mixw, **shift)
        return _rwkv_proj(xr, xk, xv, projw[0], hw, ha, hg, *projw[1:])

    r, lw, k, v, a, g = project(x2p, seq_len=T)
    nb = 2 if B % 2 == 0 else 1
    seq = lambda o: o.reshape(B, T, D)
    y_p, s_p = _wkv_seq(seq(r), seq(lw), seq(k), seq(v), seq(a),
                        *[_head_param(p, nb, hd) for p in head_pars])
    wkv_p = s_p.reshape(hd, hd, B, H).transpose(2, 3, 1, 0)
    x3p = _rwkv_out(y_p.reshape(n_p, D), g, wo_bf, x2p, lg10, lb10)

    xprev_s = jnp.concatenate([state_rwkv_shift[0], x2s[:(Ts - 1) * Bs]], axis=0)
    r, lw, k, v, a, g = project(x2s, xp=xprev_s)
    s0_s = state_rwkv_wkv[0].transpose(3, 2, 0, 1).reshape(hd, hd, Bs * H)
    y_s, s_s = _wkv(*[_to_lanes(o, Ts, Bs, hd) for o in (r, lw, k, v, a)], s0_s,
                    *[_head_param(p, Bs, hd) for p in head_pars])
    wkv_s = s_s.reshape(hd, hd, Bs, H).transpose(2, 3, 1, 0)
    x3s = _rwkv_out(_from_lanes(y_s, Ts, Bs, hd), g, wo_bf, x2s, lg10, lb10)

    peer1 = (bf(peer_w_q[1]), peer_keys[1], *_expert_tables(peer_u, peer_v, 1),
             row(ln_g[1, 1]), row(ln_b[1, 1]))
    y_prompt = _peer_layer(x3p, *peer1).reshape(B, T, D)
    y_sample = _peer_layer(x3s, *peer1).reshape(Ts, Bs, D).transpose(1, 0, 2)
    return (y_prompt, y_sample, pool_p, pool_s, shift_p, shift_s, wkv_p[None], wkv_s[None])
```
